```python
import math
import jax
import jax.numpy as jnp
from jax import lax
import numpy as np

D_MODEL = 1024
BATCH = 8
SEQ = 4096
DEPTH = 4
DEC_BATCH = 16
DEC_SEQ = 4096
PAST_LEN = 128

N_MIXERS = 3
N_LAYERS_A = (DEPTH + 2) // 3
N_LAYERS_B = (DEPTH + 1) // 3
N_LAYERS_C = DEPTH // 3

NORM_EPS = 1e-6
D_FF = 2816
MEM_LEN = 256
XA_HEADS = 4
XA_HEAD_DIM = D_MODEL // XA_HEADS

SSD_D_INNER = 2 * D_MODEL
SSD_HEAD_DIM = 64
SSD_HEADS = SSD_D_INNER // SSD_HEAD_DIM
SSD_GROUPS = 4
SSD_HEADS_PER_GROUP = SSD_HEADS // SSD_GROUPS
SSD_STATE = 128
SSD_CONV_K = 5
SSD_CONV_DIM = SSD_D_INNER + 2 * SSD_GROUPS * SSD_STATE
SSD_IN_DIM = SSD_D_INNER + SSD_CONV_DIM + 2 * SSD_HEADS
SSD_CHUNK = 128
SSD_NORM_EPS = 1e-5

RWKV_HEAD_DIM = 64
RWKV_HEADS = D_MODEL // RWKV_HEAD_DIM
RWKV_DECAY_LORA = 64
RWKV_AAA_LORA = 64
RWKV_GATE_LORA = 128
RWKV_GN_EPS = 64e-5

RET_HEADS = 4
RET_QK_DIM = D_MODEL // RET_HEADS
RET_V_DIM = 2 * RET_QK_DIM
RET_V_TOTAL = RET_HEADS * RET_V_DIM
RET_IN_DIM = 2 * D_MODEL + 2 * RET_V_TOTAL
RET_CHUNK = 128
RET_GN_EPS = 1e-6
ROPE_BASE = 10000.0

kernel_name = "hybrid_bidir_ssd_rwkv7_retention_encoder"


def rmsnorm(x, g, eps=NORM_EPS):
    xf = x.astype(jnp.float32)
    y = xf * lax.rsqrt(jnp.mean(xf * xf, axis=-1, keepdims=True) + eps)
    return (y * g.astype(jnp.float32)).astype(x.dtype)


def head_groupnorm(x, g, b, eps):
    xf = x.astype(jnp.float32)
    mu = jnp.mean(xf, axis=-1, keepdims=True)
    xc = xf - mu
    y = xc * lax.rsqrt(jnp.mean(xc * xc, axis=-1, keepdims=True) + eps)
    y = y.reshape(*x.shape[:-2], -1)
    return y * g.astype(jnp.float32) + b.astype(jnp.float32)


def swiglu(x, w_in, w_out):
    gate, up = jnp.split(x @ w_in, 2, axis=-1)
    return (jax.nn.silu(gate) * up) @ w_out


def cross_attention(x, mem, wq, wkv, wo):
    b, s, _ = x.shape
    m = mem.shape[1]
    q = (x @ wq).reshape(b, s, XA_HEADS, XA_HEAD_DIM)
    k, v = jnp.split(mem @ wkv, 2, axis=-1)
    k = k.reshape(b, m, XA_HEADS, XA_HEAD_DIM)
    v = v.reshape(b, m, XA_HEADS, XA_HEAD_DIM)
    scores = jnp.einsum('bshd,bmhd->bhsm', q, k).astype(jnp.float32) * (XA_HEAD_DIM ** -0.5)
    p = jax.nn.softmax(scores, axis=-1).astype(v.dtype)
    o = jnp.einsum('bhsm,bmhd->bshd', p, v).reshape(b, s, D_MODEL)
    return o @ wo


def centred_depthwise_conv(x, w, bias):
    k = w.shape[0]
    out = lax.conv_general_dilated(
        x, w[:, None, :].astype(x.dtype), window_strides=(1,),
        padding=[(k // 2, k // 2)], dimension_numbers=('NWC', 'WIO', 'NWC'),
        feature_group_count=x.shape[-1])
    return out + bias.astype(x.dtype)


def ssd_chunked_scan(x, a, b_mat, c_mat):
    bsz, l, g, r, p = x.shape
    n = b_mat.shape[-1]
    nc = l // SSD_CHUNK
    causal = jnp.tril(jnp.ones((SSD_CHUNK, SSD_CHUNK), dtype=bool))

    def to_chunks(t):
        return jnp.moveaxis(t.reshape(bsz, nc, SSD_CHUNK, *t.shape[2:]), 1, 0)

    def step(h, inp):
        x_c, b_c, c_c, a_c = inp
        a_cum = jnp.cumsum(a_c, axis=1)
        seg = a_cum[:, :, None] - a_cum[:, None, :]
        decay = jnp.exp(jnp.where(causal[None, :, :, None, None], seg, -jnp.inf))
        cb = jnp.einsum('btgn,bsgn->btsg', c_c, b_c)
        y_diag = jnp.einsum('btsg,btsgr,bsgrp->btgrp', cb, decay, x_c)
        y_off = jnp.einsum('btgn,bgrpn,btgr->btgrp', c_c, h, jnp.exp(a_cum))
        a_tot = a_cum[:, -1]
        h = h * jnp.exp(a_tot)[:, :, :, None, None] + jnp.einsum(
            'bsgn,bsgr,bsgrp->bgrpn', b_c, jnp.exp(a_tot[:, None] - a_cum), x_c)
        return h, y_diag + y_off

    h0 = jnp.zeros((bsz, g, r, p, n), jnp.float32)
    _, y = lax.scan(step, h0, (to_chunks(x), to_chunks(b_mat), to_chunks(c_mat),
                               to_chunks(a.astype(jnp.float32))))
    return jnp.moveaxis(y, 0, 1).reshape(bsz, l, g, r, p)


def ssd_mixer(x, w_in, conv_w, conv_b, dt_bias, a_log, d_skip, norm_g, w_out):
    bsz, l, _ = x.shape
    G, R, P, N = SSD_GROUPS, SSD_HEADS_PER_GROUP, SSD_HEAD_DIM, SSD_STATE
    z, xbc, dt = jnp.split(x @ w_in, [SSD_D_INNER, SSD_D_INNER + SSD_CONV_DIM], axis=-1)
    xbc = jax.nn.silu(centred_depthwise_conv(xbc, conv_w, conv_b))
    xs, bm, cm = jnp.split(xbc, [SSD_D_INNER, SSD_D_INNER + G * N], axis=-1)
    xs = xs.reshape(bsz, l, G, R, P)
    bm = bm.reshape(bsz, l, G, N)
    cm = cm.reshape(bsz, l, G, N)
    dt = jax.nn.softplus(dt.astype(jnp.float32).reshape(bsz, l, 2, G, R)
                         + dt_bias.astype(jnp.float32).reshape(2, G, R))
    a = -jnp.exp(a_log.astype(jnp.float32)).reshape(2, G, R)
    dt_f, dt_b = dt[:, :, 0], dt[:, :, 1]
    y_f = ssd_chunked_scan(xs * dt_f[..., None], dt_f * a[0], bm, cm)
    y_b = ssd_chunked_scan((xs * dt_b[..., None])[:, ::-1], (dt_b * a[1])[:, ::-1],
                           bm[:, ::-1], cm[:, ::-1])[:, ::-1]
    y = y_f + y_b + xs * d_skip.reshape(G, R)[..., None]
    y = (y.reshape(bsz, l, SSD_D_INNER) * jax.nn.silu(z)).reshape(bsz, l, G, -1)
    y = rmsnorm(y, norm_g.reshape(G, -1), SSD_NORM_EPS).reshape(bsz, l, SSD_D_INNER)
    return y @ w_out


def rwkv7_scan(r, decay, k, v, kk, a, reverse):
    bsz, l, h, n = r.shape

    def step(S, inp):
        r_t, w_t, k_t, v_t, kk_t, a_t = inp
        s_kk = jnp.einsum('bhvk,bhk->bhv', S, kk_t)
        S = (S * w_t[:, :, None, :]
             - s_kk[..., None] * (kk_t * a_t)[:, :, None, :]
             + v_t[..., None] * k_t[:, :, None, :])
        return S, jnp.einsum('bhvk,bhk->bhv', S, r_t)

    seq = tuple(jnp.swapaxes(t.astype(jnp.float32), 0, 1) for t in (r, decay, k, v, kk, a))
    S0 = jnp.zeros((bsz, h, n, n), jnp.float32)
    _, y = lax.scan(step, S0, seq, reverse=reverse)
    return jnp.swapaxes(y, 0, 1)


def rwkv7_mixer(x, mix, w_rkv, w0, w1, w2, a0, a1, a2, g1, g2, k_k, k_a, r_k, ln_g, ln_b, w_out):
    bsz, l, _ = x.shape
    H, N = RWKV_HEADS, RWKV_HEAD_DIM

    def heads(t):
        return t.reshape(bsz, l, H, N)

    x_prev = jnp.pad(x, ((0, 0), (1, 0), (0, 0)))[:, :-1]
    x_next = jnp.pad(x, ((0, 0), (0, 1), (0, 0)))[:, 1:]
    xx = 0.5 * (x_prev + x_next) - x
    xm = x[:, :, None, :] + xx[:, :, None, :] * mix
    rkv = jnp.einsum('blid,ide->blie', xm[:, :, :3], w_rkv)
    r, k, v = rkv[:, :, 0], rkv[:, :, 1], rkv[:, :, 2]
    xw, xa, xg = xm[:, :, 3], xm[:, :, 4], xm[:, :, 5]
    g = jax.nn.sigmoid(xg @ g1) @ g2
    kk = heads((k * k_k).astype(jnp.float32))
    kk = kk / jnp.maximum(jnp.sqrt(jnp.sum(kk * kk, axis=-1, keepdims=True)), 1e-12)
    r_h, v_h = heads(r), heads(v)

    def direction(d, reverse):
        wpre = (w0[d] + jnp.tanh(xw @ w1[d]) @ w2[d]).astype(jnp.float32)
        decay = jnp.exp(-jnp.exp(-jax.nn.softplus(-wpre) - 0.5))
        a = jax.nn.sigmoid(a0[d] + (xa @ a1[d]) @ a2[d])
        k_d = heads(k * (1.0 + (a - 1.0) * k_a))
        y = rwkv7_scan(r_h, heads(decay), k_d, v_h, kk, heads(a), reverse)
        bonus = jnp.sum(r_h * k_d * r_k, axis=-1, keepdims=True) * v_h
        return y, bonus

    y_f, bonus_f = direction(0, False)
    y_b, bonus_b = direction(1, True)
    y = head_groupnorm(y_f + y_b, ln_g, ln_b, RWKV_GN_EPS)
    y = y + (bonus_f + bonus_b).reshape(bsz, l, D_MODEL)
    return (y * g) @ w_out


def rotary(x):
    l, d = x.shape[1], x.shape[-1]
    inv = ROPE_BASE ** (-jnp.arange(0, d, 2, dtype=jnp.float32) / d)
    ang = jnp.arange(l, dtype=jnp.float32)[:, None] * inv[None, :]
    cos = jnp.cos(ang)[None, :, None, :]
    sin = jnp.sin(ang)[None, :, None, :]
    x1 = x[..., 0::2].astype(jnp.float32)
    x2 = x[..., 1::2].astype(jnp.float32)
    out = jnp.stack([x1 * cos - x2 * sin, x1 * sin + x2 * cos], axis=-1).reshape(x.shape)
    return out.astype(x.dtype)


def retention_chunked(q, k, v, log_gamma, include_diag):
    bsz, l, h, dk = q.shape
    dv = v.shape[-1]
    nc = l // RET_CHUNK
    idx = jnp.arange(RET_CHUNK, dtype=jnp.float32)
    diff = idx[:, None] - idx[None, :]
    mask = (diff >= 0) if include_diag else (diff > 0)
    inner_decay = jnp.where(mask[None], jnp.exp(log_gamma[:, None, None] * jnp.maximum(diff, 0.0)[None]), 0.0)
    q_decay = jnp.exp(log_gamma[None, :] * (idx[:, None] + 1.0))
    k_decay = jnp.exp(log_gamma[None, :] * (RET_CHUNK - 1.0 - idx[:, None]))
    chunk_gamma = jnp.exp(log_gamma * RET_CHUNK)

    def to_chunks(t):
        return jnp.moveaxis(t.reshape(bsz, nc, RET_CHUNK, *t.shape[2:]), 1, 0)

    def step(Rs, inp):
        q_c, k_c, v_c = inp
        scores = jnp.einsum('bthd,bshd->bhts', q_c, k_c) * inner_decay
        inner = jnp.einsum('bhts,bshv->bthv', scores, v_c)
        cross = jnp.einsum('bthd,bhdv->bthv', q_c, Rs) * q_decay[None, :, :, None]
        Rs = Rs * chunk_gamma[None, :, None, None] + jnp.einsum(
            'bshd,bshv->bhdv', k_c * k_decay[None, :, :, None], v_c)
        return Rs, inner + cross

    R0 = jnp.zeros((bsz, h, dk, dv), jnp.float32)
    _, y = lax.scan(step, R0, (to_chunks(q), to_chunks(k), to_chunks(v)))
    return jnp.moveaxis(y, 0, 1).reshape(bsz, l, h, dv)


def retention_mixer(x, w_in, gn_g, gn_b, w_out):
    bsz, l, _ = x.shape
    H = RET_HEADS
    q, k, v, g = jnp.split(x @ w_in, [D_MODEL, 2 * D_MODEL, 2 * D_MODEL + RET_V_TOTAL], axis=-1)
    q = rotary(q.reshape(bsz, l, H, RET_QK_DIM)) * (RET_QK_DIM ** -0.5)
    k = rotary(k.reshape(bsz, l, H, RET_QK_DIM))
    v = v.reshape(bsz, l, H, RET_V_DIM)
    log_gamma = jnp.log1p(-jnp.exp2(-5.0 - jnp.arange(H, dtype=jnp.float32)))
    fwd = retention_chunked(q, k, v, log_gamma, True)
    bwd = retention_chunked(q[:, ::-1], k[:, ::-1], v[:, ::-1], log_gamma, False)[:, ::-1]
    y = head_groupnorm(fwd + bwd, gn_g, gn_b, RET_GN_EPS)
    return (jax.nn.silu(g) * y) @ w_out


def trunk(x, mem, norm_g, mem_norm_g, ffn_w_in, ffn_w_out, xa_wq, xa_wkv, xa_wo,
          ssd_params, rwkv_params, ret_params):
    for i in range(DEPTH):
        ng = norm_g[i]
        x = x + 0.5 * rmsnorm(swiglu(rmsnorm(x, ng[0]), ffn_w_in[i, 0], ffn_w_out[i, 0]), ng[1])
        h = rmsnorm(x, ng[2])
        kind, j = i % N_MIXERS, i // N_MIXERS
        if kind == 0:
            h = ssd_mixer(h, *[p[j] for p in ssd_params])
        elif kind == 1:
            h = rwkv7_mixer(h, *[p[j] for p in rwkv_params])
        else:
            h = retention_mixer(h, *[p[j] for p in ret_params])
        x = x + rmsnorm(h, ng[3])
        m = rmsnorm(mem, mem_norm_g[i])
        x = x + rmsnorm(cross_attention(rmsnorm(x, ng[4]), m, xa_wq[i], xa_wkv[i], xa_wo[i]), ng[5])
        x = x + 0.5 * rmsnorm(swiglu(rmsnorm(x, ng[6]), ffn_w_in[i, 1], ffn_w_out[i, 1]), ng[7])
    return x


def setup_inputs(seed: int = 0) -> dict:
    key = jax.random.key(seed)
    ks = iter(jax.random.split(key, 64))

    def nrm(shape, scale):
        return jax.random.normal(next(ks), shape, jnp.float32) * scale

    def gain(shape):
        return 1.0 + nrm(shape, 0.02)

    D = D_MODEL
    NA, NB, NC = N_LAYERS_A, N_LAYERS_B, N_LAYERS_C
    inp = {}
    inp['x_prompt'] = nrm((BATCH, SEQ, D), 1.0)
    inp['x_sample'] = nrm((DEC_BATCH, DEC_SEQ, D), 1.0)
    inp['mem_prompt'] = nrm((BATCH, MEM_LEN, D), 1.0)
    inp['mem_sample'] = nrm((DEC_BATCH, MEM_LEN, D), 1.0)
    inp['norm_g'] = gain((DEPTH, 8, D))
    inp['mem_norm_g'] = gain((DEPTH, D))
    inp['ffn_w_in'] = nrm((DEPTH, 2, D, 2 * D_FF), D ** -0.5)
    inp['ffn_w_out'] = nrm((DEPTH, 2, D_FF, D), D_FF ** -0.5)
    inp['xa_wq'] = nrm((DEPTH, D, D), D ** -0.5)
    inp['xa_wkv'] = nrm((DEPTH, D, 2 * D), D ** -0.5)
    inp['xa_wo'] = nrm((DEPTH, D, D), D ** -0.5)
    inp['ssd_w_in'] = nrm((NA, D, SSD_IN_DIM), D ** -0.5)
    inp['ssd_conv_w'] = nrm((NA, SSD_CONV_K, SSD_CONV_DIM), SSD_CONV_K ** -0.5)
    inp['ssd_conv_b'] = nrm((NA, SSD_CONV_DIM), 0.02)
    dt = jnp.exp(jax.random.uniform(next(ks), (NA, 2, SSD_HEADS), jnp.float32,
                                    minval=math.log(1e-3), maxval=math.log(1e-1)))
    inp['ssd_dt_bias'] = dt + jnp.log(-jnp.expm1(-dt))
    inp['ssd_a_log'] = jnp.log(jax.random.uniform(next(ks), (NA, 2, SSD_HEADS), jnp.float32,
                                                  minval=1.0, maxval=16.0))
    inp['ssd_d'] = 1.0 + nrm((NA, SSD_HEADS), 0.1)
    inp['ssd_norm_g'] = gain((NA, SSD_D_INNER))
    inp['ssd_w_out'] = nrm((NA, SSD_D_INNER, D), SSD_D_INNER ** -0.5)
    inp['rwkv_mix'] = jax.random.uniform(next(ks), (NB, 6, D), jnp.float32)
    inp['rwkv_w_rkv'] = nrm((NB, 3, D, D), D ** -0.5)
    inp['rwkv_w0'] = jax.random.uniform(next(ks), (NB, 2, D), jnp.float32, minval=-6.0, maxval=-1.0)
    inp['rwkv_w1'] = nrm((NB, 2, D, RWKV_DECAY_LORA), D ** -0.5)
    inp['rwkv_w2'] = nrm((NB, 2, RWKV_DECAY_LORA, D), 0.1 * RWKV_DECAY_LORA ** -0.5)
    inp['rwkv_a0'] = nrm((NB, 2, D), 0.1)
    inp['rwkv_a1'] = nrm((NB, 2, D, RWKV_AAA_LORA), D ** -0.5)
    inp['rwkv_a2'] = nrm((NB, 2, RWKV_AAA_LORA, D), 0.1 * RWKV_AAA_LORA ** -0.5)
    inp['rwkv_g1'] = nrm((NB, D, RWKV_GATE_LORA), D ** -0.5)
    inp['rwkv_g2'] = nrm((NB, RWKV_GATE_LORA, D), RWKV_GATE_LORA ** -0.5)
    inp['rwkv_k_k'] = 0.85 + nrm((NB, D), 0.05)
    inp['rwkv_k_a'] = 1.0 + nrm((NB, D), 0.05)
    inp['rwkv_r_k'] = nrm((NB, RWKV_HEADS, RWKV_HEAD_DIM), 0.1)
    inp['rwkv_ln_g'] = gain((NB, D))
    inp['rwkv_ln_b'] = nrm((NB, D), 0.02)
    inp['rwkv_w_out'] = nrm((NB, D, D), D ** -0.5)
    inp['ret_w_in'] = nrm((NC, D, RET_IN_DIM), D ** -0.5)
    inp['ret_gn_g'] = gain((NC, RET_V_TOTAL))
    inp['ret_gn_b'] = nrm((NC, RET_V_TOTAL), 0.02)
    inp['ret_w_out'] = nrm((NC, RET_V_TOTAL, D), RET_V_TOTAL ** -0.5)
    return inp


def reference(x_prompt, x_sample, mem_prompt, mem_sample, norm_g, mem_norm_g,
              ffn_w_in, ffn_w_out, xa_wq, xa_wkv, xa_wo,
              ssd_w_in, ssd_conv_w, ssd_conv_b, ssd_dt_bias, ssd_a_log, ssd_d, ssd_norm_g, ssd_w_out,
              rwkv_mix, rwkv_w_rkv, rwkv_w0, rwkv_w1, rwkv_w2, rwkv_a0, rwkv_a1, rwkv_a2,
              rwkv_g1, rwkv_g2, rwkv_k_k, rwkv_k_a, rwkv_r_k, rwkv_ln_g, rwkv_ln_b, rwkv_w_out,
              ret_w_in, ret_gn_g, ret_gn_b, ret_w_out):
    ssd_params = (ssd_w_in, ssd_conv_w, ssd_conv_b, ssd_dt_bias, ssd_a_log, ssd_d, ssd_norm_g, ssd_w_out)
    rwkv_params = (rwkv_mix, rwkv_w_rkv, rwkv_w0, rwkv_w1, rwkv_w2, rwkv_a0, rwkv_a1, rwkv_a2,
                   rwkv_g1, rwkv_g2, rwkv_k_k, rwkv_k_a, rwkv_r_k, rwkv_ln_g, rwkv_ln_b, rwkv_w_out)
    ret_params = (ret_w_in, ret_gn_g, ret_gn_b, ret_w_out)
    y_prompt = trunk(x_prompt, mem_prompt, norm_g, mem_norm_g, ffn_w_in, ffn_w_out,
                     xa_wq, xa_wkv, xa_wo, ssd_params, rwkv_params, ret_params)
    y_sample = trunk(x_sample, mem_sample, norm_g, mem_norm_g, ffn_w_in, ffn_w_out,
                     xa_wq, xa_wkv, xa_wo, ssd_params, rwkv_params, ret_params)
    return (y_prompt, y_sample)
```

```python
import functools
import math

import jax
import jax.numpy as jnp
import numpy as np
from jax import lax
from jax.experimental import pallas as pl
from jax.experimental.pallas import tpu as pltpu

F32 = jnp.float32
BF16 = jnp.bfloat16

D_MODEL = 1024
DEPTH = 4
NORM_EPS = 1e-6
D_FF = 2816
XA_HEADS = 4
XA_HEAD_DIM = D_MODEL // XA_HEADS

SSD_D_INNER = 2 * D_MODEL
SSD_HEAD_DIM = 64
SSD_HEADS = SSD_D_INNER // SSD_HEAD_DIM
SSD_GROUPS = 4
SSD_HPG = SSD_HEADS // SSD_GROUPS
SSD_GW = SSD_HPG * SSD_HEAD_DIM
SSD_STATE = 128
SSD_CONV_K = 5
SSD_CONV_DIM = SSD_D_INNER + 2 * SSD_GROUPS * SSD_STATE
SSD_CHUNK = 128
SSD_NORM_EPS = 1e-5

RWKV_HEAD_DIM = 64
RWKV_HEADS = D_MODEL // RWKV_HEAD_DIM
RWKV_GN_EPS = 64e-5
RWKV_CHUNK = 64

RET_HEADS = 4
RET_QK_DIM = D_MODEL // RET_HEADS
RET_V_DIM = 2 * RET_QK_DIM
RET_V_TOTAL = RET_HEADS * RET_V_DIM
RET_CHUNK = 128
RET_GN_EPS = 1e-6
ROPE_BASE = 10000.0

LANES = 128
HALO = 16
VMEM_LIMIT = 56 * 1024 * 1024


def _rms(x, g, eps):
    return x * lax.rsqrt(jnp.mean(x * x, axis=-1, keepdims=True) + eps) * g


def _sigmoid(x):
    return 1.0 / (1.0 + jnp.exp(-x))


def _softplus(x):
    return jnp.maximum(x, 0.0) + jnp.log1p(jnp.exp(-jnp.abs(x)))


def _dot(a, b):
    return jnp.dot(a, b, preferred_element_type=F32)


def _dot_nt(a, b):
    return lax.dot_general(a, b, (((1,), (1,)), ((), ())), preferred_element_type=F32)


def _dot_tn(a, b):
    return lax.dot_general(a, b, (((0,), (0,)), ((), ())), preferred_element_type=F32)


def _split3(x):
    hi = x.astype(BF16)
    r1 = x - hi.astype(F32)
    mid = r1.astype(BF16)
    lo = (r1 - mid.astype(F32)).astype(BF16)
    return hi, mid, lo


def _dot01_right(x, m01):
    hi, mid, lo = _split3(x)
    return _dot(hi, m01) + _dot(mid, m01) + _dot(lo, m01)


def _dot01_left(m01, x):
    hi, mid, lo = _split3(x)
    return _dot(m01, hi) + _dot(m01, mid) + _dot(m01, lo)


def _segsum64(x, ones_bd):
    hi, mid, lo = _split3(x)
    outs = []
    for j in range(x.shape[-1] // 256):
        sl = slice(j * 256, (j + 1) * 256)
        outs.append(_dot(hi[:, sl], ones_bd) + _dot(mid[:, sl], ones_bd) + _dot(lo[:, sl], ones_bd))
    return jnp.concatenate(outs, axis=-1)


def _const_spec(shape):
    nd = len(shape)
    return pl.BlockSpec(shape, lambda *_: (0,) * nd)


def _resident_spec(shape):
    nd = len(shape)
    return pl.BlockSpec(shape, lambda *_: (0,) * nd, pipeline_mode=pl.Buffered(1))


def _params(*sem):
    return pltpu.CompilerParams(dimension_semantics=sem, vmem_limit_bytes=VMEM_LIMIT)


def _tile(n, want):
    t = min(n, want)
    assert n % t == 0, (n, t)
    return t


def _halo_specs(tm, l, d):
    per = tm // HALO
    last = l // HALO - 1
    prev = pl.BlockSpec((1, HALO, d), lambda b, i: (b, jnp.maximum(i * per - 1, 0), 0))
    nxt = pl.BlockSpec((1, HALO, d), lambda b, i: (b, jnp.minimum((i + 1) * per, last), 0))
    return prev, nxt


FFN_COLS = 256


def _ffn_body(x_ref, g_ref, win_ref, wout_ref, o_ref, acc_ref):
    x = x_ref[...]
    h = _rms(x, g_ref[0:1, :], NORM_EPS).astype(BF16)
    for c in range(D_FF // FFN_COLS):
        lo, hi = c * FFN_COLS, (c + 1) * FFN_COLS
        gate = _dot(h, win_ref[:, lo:hi])
        up = _dot(h, win_ref[:, D_FF + lo:D_FF + hi])
        act = (gate * _sigmoid(gate) * up).astype(BF16)
        part = _dot(act, wout_ref[lo:hi, :])
        if c == 0:
            acc_ref[...] = part
        else:
            acc_ref[...] += part
    o_ref[...] = x + 0.5 * _rms(acc_ref[...], g_ref[1:2, :], NORM_EPS)


def _ffn(x, g2, w_in, w_out):
    b, l, d = x.shape
    t = b * l
    tm = _tile(t, 512)
    out = pl.pallas_call(
        _ffn_body,
        out_shape=jax.ShapeDtypeStruct((t, d), F32),
        grid=(t // tm,),
        in_specs=[pl.BlockSpec((tm, d), lambda i: (i, 0)),
                  _const_spec((2, d)),
                  _resident_spec(w_in.shape),
                  _resident_spec(w_out.shape)],
        out_specs=pl.BlockSpec((tm, d), lambda i: (i, 0)),
        scratch_shapes=[pltpu.VMEM((tm, d), F32)],
        compiler_params=_params("parallel"),
        name="ffn",
    )(x.reshape(t, d), g2, w_in, w_out)
    return out.reshape(b, l, d)


def _memkv_body(m_ref, g_ref, w_ref, o_ref):
    h = _rms(m_ref[...], g_ref[...], NORM_EPS).astype(BF16)
    o_ref[...] = _dot(h, w_ref[...]).astype(BF16)


def _memkv(mem, g, wkv):
    b, m, d = mem.shape
    t = b * m
    tm = _tile(t, 256)
    out = pl.pallas_call(
        _memkv_body,
        out_shape=jax.ShapeDtypeStruct((t, 2 * d), BF16),
        grid=(t // tm,),
        in_specs=[pl.BlockSpec((tm, d), lambda i: (i, 0)),
                  _const_spec((1, d)),
                  _resident_spec(wkv.shape)],
        out_specs=pl.BlockSpec((tm, 2 * d), lambda i: (i, 0)),
        compiler_params=_params("parallel"),
        name="memkv",
    )(mem.reshape(t, d), g, wkv)
    return out.reshape(b, m, 2 * d)


def _xattn_body(x_ref, g_ref, kv_ref, wq_ref, wo_ref, o_ref):
    x = x_ref[0]
    h = _rms(x, g_ref[0:1, :], NORM_EPS).astype(BF16)
    q = _dot(h, wq_ref[...])
    scale = XA_HEAD_DIM ** -0.5
    outs = []
    for hd in range(XA_HEADS):
        lo, hi = hd * XA_HEAD_DIM, (hd + 1) * XA_HEAD_DIM
        s = _dot_nt(q[:, lo:hi].astype(BF16), kv_ref[0, :, lo:hi]) * scale
        s = s - jnp.max(s, axis=-1, keepdims=True)
        p = jnp.exp(s)
        p = p / jnp.sum(p, axis=-1, keepdims=True)
        outs.append(_dot(p.astype(BF16), kv_ref[0, :, D_MODEL + lo:D_MODEL + hi]))
    o = jnp.concatenate(outs, axis=-1).astype(BF16)
    y = _dot(o, wo_ref[...])
    o_ref[0] = x + _rms(y, g_ref[1:2, :], NORM_EPS)


def _xattn(x, g2, kv, wq, wo):
    b, l, d = x.shape
    m = kv.shape[1]
    tm = _tile(l, 512)
    return pl.pallas_call(
        _xattn_body,
        out_shape=jax.ShapeDtypeStruct((b, l, d), F32),
        grid=(b, l // tm),
        in_specs=[pl.BlockSpec((1, tm, d), lambda i, j: (i, j, 0)),
                  _const_spec((2, d)),
                  pl.BlockSpec((1, m, 2 * d), lambda i, j: (i, 0, 0)),
                  _resident_spec(wq.shape),
                  _resident_spec(wo.shape)],
        out_specs=pl.BlockSpec((1, tm, d), lambda i, j: (i, j, 0)),
        compiler_params=_params("parallel", "parallel"),
        name="xattn",
    )(x, g2, kv, wq, wo)


def _ssd_in_body(xc_ref, xp_ref, xn_ref, g_ref, wz_ref, wx_ref, wdt_ref, cw_ref, cb_ref, dtb_ref,
                 z_ref, xs_ref, bm_ref, cm_ref, dt_ref, ext_ref):
    i = pl.program_id(1)
    nt = pl.num_programs(1)
    tm = xc_ref.shape[1]
    g = g_ref[...]
    hc = _rms(xc_ref[0], g, NORM_EPS).astype(BF16)
    hp = _rms(xp_ref[0], g, NORM_EPS).astype(BF16)
    hn = _rms(xn_ref[0], g, NORM_EPS).astype(BF16)
    z_ref[0] = _dot(hc, wz_ref[...])
    dt = _softplus(_dot(hc, wdt_ref[...]) + dtb_ref[...])
    for k in range(2 * SSD_GROUPS):
        dt_ref[0, k] = dt[:, k * LANES:(k + 1) * LANES]
    ext_ref[0:HALO, :] = jnp.where(i > 0, _dot(hp, wx_ref[...]), 0.0)
    ext_ref[HALO:HALO + tm, :] = _dot(hc, wx_ref[...])
    ext_ref[HALO + tm:2 * HALO + tm, :] = jnp.where(i < nt - 1, _dot(hn, wx_ref[...]), 0.0)
    acc = jnp.broadcast_to(cb_ref[...], (tm, SSD_CONV_DIM))
    for j in range(SSD_CONV_K):
        off = HALO - SSD_CONV_K // 2 + j
        acc = acc + cw_ref[j:j + 1, :] * ext_ref[off:off + tm, :]
    y = acc * _sigmoid(acc)
    for gi in range(SSD_GROUPS):
        xs_ref[0, gi] = y[:, gi * SSD_GW:(gi + 1) * SSD_GW]
        bm_ref[0, gi] = y[:, SSD_D_INNER + gi * SSD_STATE:SSD_D_INNER + (gi + 1) * SSD_STATE]
        cm_ref[0, gi] = y[:, SSD_D_INNER + (SSD_GROUPS + gi) * SSD_STATE:
                          SSD_D_INNER + (SSD_GROUPS + gi + 1) * SSD_STATE]


def _ssd_in(x, g, wz, wx, wdt, conv_w, conv_b, dt_bias):
    b, l, d = x.shape
    tm = _tile(l, 256)
    prev, nxt = _halo_specs(tm, l, d)
    G = SSD_GROUPS
    return pl.pallas_call(
        _ssd_in_body,
        out_shape=(jax.ShapeDtypeStruct((b, l, SSD_D_INNER), F32),
                   jax.ShapeDtypeStruct((b, G, l, SSD_GW), F32),
                   jax.ShapeDtypeStruct((b, G, l, SSD_STATE), F32),
                   jax.ShapeDtypeStruct((b, G, l, SSD_STATE), F32),
                   jax.ShapeDtypeStruct((b, 2 * G, l, LANES), F32)),
        grid=(b, l // tm),
        in_specs=[pl.BlockSpec((1, tm, d), lambda i, j: (i, j, 0)), prev, nxt,
                  _const_spec((1, d)),
                  _resident_spec(wz.shape), _resident_spec(wx.shape), _resident_spec(wdt.shape),
                  _const_spec(conv_w.shape), _const_spec(conv_b.shape), _const_spec(dt_bias.shape)],
        out_specs=(pl.BlockSpec((1, tm, SSD_D_INNER), lambda i, j: (i, j, 0)),
                   pl.BlockSpec((1, G, tm, SSD_GW), lambda i, j: (i, 0, j, 0)),
                   pl.BlockSpec((1, G, tm, SSD_STATE), lambda i, j: (i, 0, j, 0)),
                   pl.BlockSpec((1, G, tm, SSD_STATE), lambda i, j: (i, 0, j, 0)),
                   pl.BlockSpec((1, 2 * G, tm, LANES), lambda i, j: (i, 0, j, 0))),
        scratch_shapes=[pltpu.VMEM((tm + 2 * HALO, SSD_CONV_DIM), F32)],
        compiler_params=_params("parallel", "parallel"),
        name="ssd_in",
    )(x, x, x, g, wz, wx, wdt, conv_w, conv_b, dt_bias)


def _ssd_scan_body(xs_ref, bm_ref, cm_ref, dt_ref, alog_ref, dskip_ref, tm_ref, tmt_ref,
                   y_ref, h_ref):
    d = pl.program_id(1)
    c = pl.program_id(2)
    Q = SSD_CHUNK
    P = SSD_HEAD_DIM

    @pl.when(c == 0)
    def _():
        h_ref[...] = jnp.zeros_like(h_ref)

    tmat = tm_ref[0]
    tmat_t = tmt_ref[0]
    mask = tmat > 0
    lane = lax.broadcasted_iota(jnp.int32, (Q, 2 * P), 1)
    low = lane < P

    def group(gi, carry):
        xs = xs_ref[0, gi]
        bm = bm_ref[0, gi]
        cm = cm_ref[0, gi]
        dt = dt_ref[0, gi]
        a = dt * (-jnp.exp(alog_ref[0, gi]))
        cum = _dot01_left(tmat, a)
        cum_row = _dot01_right(a.T, tmat_t)
        cb = _dot_nt(cm.astype(BF16), bm.astype(BF16))
        hst = h_ref[gi]
        yoff = _dot(cm.astype(BF16), hst.astype(BF16))
        dsk = dskip_ref[0, gi]
        xw_parts = []
        for pr in range(SSD_HPG // 2):
            h0, h1 = 2 * pr, 2 * pr + 1
            sl = slice(pr * 2 * P, (pr + 1) * 2 * P)
            bc0 = jnp.broadcast_to(cum[:, h0:h0 + 1], (Q, Q))
            bc1 = jnp.broadcast_to(cum[:, h1:h1 + 1], (Q, Q))
            m0 = cb * jnp.exp(jnp.where(mask, bc0 - cum_row[h0:h0 + 1, :], -jnp.inf))
            m1 = cb * jnp.exp(jnp.where(mask, bc1 - cum_row[h1:h1 + 1, :], -jnp.inf))
            cum_pair = jnp.where(low, bc0, bc1)
            dt_pair = jnp.where(low, jnp.broadcast_to(dt[:, h0:h0 + 1], (Q, 2 * P)),
                                jnp.broadcast_to(dt[:, h1:h1 + 1], (Q, 2 * P)))
            tot_pair = jnp.where(d == 0, cum_pair[Q - 1:Q, :], cum_pair[0:1, :])
            xs_pair = xs[:, sl]
            xdt = xs_pair * dt_pair
            rhs = jnp.concatenate([jnp.where(low, xdt, 0.0), jnp.where(low, 0.0, xdt)], axis=0)
            ydiag = _dot(jnp.concatenate([m0, m1], axis=1).astype(BF16), rhs.astype(BF16))
            y_pair = ydiag + yoff[:, sl] * jnp.exp(cum_pair) + xs_pair * dsk[:, sl]
            y_ref[0, 0, gi, :, sl] = y_pair
            xw_parts.append((xdt * jnp.exp(tot_pair - cum_pair)).astype(BF16))
            h_ref[gi, :, sl] = hst[:, sl] * jnp.exp(tot_pair)
        xw = jnp.concatenate(xw_parts, axis=1)
        h_ref[gi] = h_ref[gi] + _dot(bm.T.astype(BF16), xw)
        return carry

    lax.fori_loop(0, SSD_GROUPS, group, 0)


def _ssd_scan(xs, bm, cm, dt, alog, dskip, tmats):
    b, G, l, _ = xs.shape
    Q = SSD_CHUNK
    nc = l // Q

    def cc(d, c):
        return c + d * (nc - 1 - 2 * c)

    return pl.pallas_call(
        _ssd_scan_body,
        out_shape=jax.ShapeDtypeStruct((2, b, G, l, SSD_GW), F32),
        grid=(b, 2, nc),
        in_specs=[pl.BlockSpec((1, G, Q, SSD_GW), lambda i, d, c: (i, 0, cc(d, c), 0)),
                  pl.BlockSpec((1, G, Q, SSD_STATE), lambda i, d, c: (i, 0, cc(d, c), 0)),
                  pl.BlockSpec((1, G, Q, SSD_STATE), lambda i, d, c: (i, 0, cc(d, c), 0)),
                  pl.BlockSpec((1, G, Q, LANES), lambda i, d, c: (i, d, cc(d, c), 0)),
                  pl.BlockSpec((1, G, 1, LANES), lambda i, d, c: (d, 0, 0, 0)),
                  pl.BlockSpec((1, G, 1, SSD_GW), lambda i, d, c: (d, 0, 0, 0)),
                  pl.BlockSpec((1, Q, Q), lambda i, d, c: (d, 0, 0)),
                  pl.BlockSpec((1, Q, Q), lambda i, d, c: (1 - d, 0, 0))],
        out_specs=pl.BlockSpec((1, 1, G, Q, SSD_GW), lambda i, d, c: (d, i, 0, cc(d, c), 0)),
        scratch_shapes=[pltpu.VMEM((G, SSD_STATE, SSD_GW), F32)],
        compiler_params=_params("parallel", "arbitrary", "arbitrary"),
        name="ssd_scan",
    )(xs, bm, cm, dt, alog, dskip, tmats, tmats)


def _ssd_out_body(y_ref, z_ref, x_ref, ng_ref, w_ref, g_ref, o_ref):
    acc = None
    for gi in range(SSD_GROUPS):
        sl = slice(gi * SSD_GW, (gi + 1) * SSD_GW)
        z = z_ref[0, :, sl]
        y = (y_ref[0, 0, gi] + y_ref[1, 0, gi]) * (z * _sigmoid(z))
        yn = _rms(y, ng_ref[:, sl], SSD_NORM_EPS).astype(BF16)
        part = _dot(yn, w_ref[sl, :])
        acc = part if acc is None else acc + part
    o_ref[0] = x_ref[0] + _rms(acc, g_ref[...], NORM_EPS)


def _ssd_out(y, z, x, norm_g, w_out, g):
    b, l, d = x.shape
    tm = _tile(l, 256)
    G = SSD_GROUPS
    return pl.pallas_call(
        _ssd_out_body,
        out_shape=jax.ShapeDtypeStruct((b, l, d), F32),
        grid=(b, l // tm),
        in_specs=[pl.BlockSpec((2, 1, G, tm, SSD_GW), lambda i, j: (0, i, 0, j, 0)),
                  pl.BlockSpec((1, tm, SSD_D_INNER), lambda i, j: (i, j, 0)),
                  pl.BlockSpec((1, tm, d), lambda i, j: (i, j, 0)),
                  _const_spec((1, SSD_D_INNER)),
                  _resident_spec(w_out.shape),
                  _const_spec((1, d))],
        out_specs=pl.BlockSpec((1, tm, d), lambda i, j: (i, j, 0)),
        compiler_params=_params("parallel", "parallel"),
        name="ssd_out",
    )(y, z, x, norm_g, w_out, g)


def _ssd_layer(x, ng, p):
    z, xs, bm, cm, dt = _ssd_in(x, ng[2:3], p["wz"], p["wx"], p["wdt"], p["conv_w"], p["conv_b"], p["dt_bias"])
    y = _ssd_scan(xs, bm, cm, dt, p["alog"], p["dskip"], p["tmats"])
    return _ssd_out(y, z, x, p["norm_g"], p["w_out"], ng[3:4])


def _ssd_prepare(j, w_in, conv_w, conv_b, dt_bias, a_log, d_skip, norm_g, w_out):
    G, R = SSD_GROUPS, SSD_HPG
    w = w_in[j]
    wz = w[:, :SSD_D_INNER].astype(BF16)
    wx = w[:, SSD_D_INNER:SSD_D_INNER + SSD_CONV_DIM].astype(BF16)
    wdt = w[:, SSD_D_INNER + SSD_CONV_DIM:].reshape(D_MODEL, 2, G, R)
    wdt = jnp.pad(wdt, ((0, 0), (0, 0), (0, 0), (0, LANES - R))).reshape(D_MODEL, 2 * G * LANES).astype(BF16)

    def per_head(v):
        return jnp.pad(v.reshape(2, G, 1, R), ((0, 0), (0, 0), (0, 0), (0, LANES - R)))

    dskip = jnp.repeat(d_skip[j].reshape(G, 1, R), SSD_HEAD_DIM, axis=-1)
    dskip = jnp.stack([dskip, jnp.zeros_like(dskip)])
    tril = jnp.tril(jnp.ones((SSD_CHUNK, SSD_CHUNK), F32))
    return dict(wz=wz, wx=wx, wdt=wdt, conv_w=conv_w[j], conv_b=conv_b[j][None, :],
                dt_bias=per_head(dt_bias[j]).reshape(1, 2 * G * LANES),
                alog=per_head(a_log[j]), dskip=dskip,
                tmats=jnp.stack([tril, tril.T]).astype(BF16),
                norm_g=norm_g[j][None, :], w_out=w_out[j].astype(BF16))


def _ret_in_body(x_ref, g_ref, w_ref, cos_ref, sin_ref, q_ref, k_ref, v_ref, gate_ref):
    h = _rms(x_ref[0], g_ref[...], NORM_EPS).astype(BF16)
    cos = cos_ref[...]
    sin = sin_ref[...]
    half = RET_QK_DIM // 2

    def rot(t, scale):
        outs = []
        for hd in range(RET_HEADS):
            x1 = t[:, hd * RET_QK_DIM:hd * RET_QK_DIM + half]
            x2 = t[:, hd * RET_QK_DIM + half:(hd + 1) * RET_QK_DIM]
            outs.append((x1 * cos - x2 * sin) * scale)
            outs.append((x1 * sin + x2 * cos) * scale)
        return jnp.concatenate(outs, axis=-1)

    q_ref[0] = rot(_dot(h, w_ref[:, 0:D_MODEL]), RET_QK_DIM ** -0.5)
    k_ref[0] = rot(_dot(h, w_ref[:, D_MODEL:2 * D_MODEL]), 1.0)
    v_ref[0] = _dot(h, w_ref[:, 2 * D_MODEL:2 * D_MODEL + RET_V_TOTAL])
    gate_ref[0] = _dot(h, w_ref[:, 2 * D_MODEL + RET_V_TOTAL:])


def _ret_in(x, g, w, cos, sin):
    b, l, d = x.shape
    tm = _tile(l, 256)
    half = RET_QK_DIM // 2
    return pl.pallas_call(
        _ret_in_body,
        out_shape=(jax.ShapeDtypeStruct((b, l, d), F32), jax.ShapeDtypeStruct((b, l, d), F32),
                   jax.ShapeDtypeStruct((b, l, RET_V_TOTAL), F32), jax.ShapeDtypeStruct((b, l, RET_V_TOTAL), F32)),
        grid=(b, l // tm),
        in_specs=[pl.BlockSpec((1, tm, d), lambda i, j: (i, j, 0)),
                  _const_spec((1, d)),
                  _resident_spec(w.shape),
                  pl.BlockSpec((tm, half), lambda i, j: (j, 0)),
                  pl.BlockSpec((tm, half), lambda i, j: (j, 0))],
        out_specs=(pl.BlockSpec((1, tm, d), lambda i, j: (i, j, 0)),
                   pl.BlockSpec((1, tm, d), lambda i, j: (i, j, 0)),
                   pl.BlockSpec((1, tm, RET_V_TOTAL), lambda i, j: (i, j, 0)),
                   pl.BlockSpec((1, tm, RET_V_TOTAL), lambda i, j: (i, j, 0))),
        compiler_params=_params("parallel", "parallel"),
        name="ret_in",
    )(x, g, w, cos, sin)


def _ret_scan_body(q_ref, k_ref, v_ref, dm_ref, qd_ref, kd_ref, cg_ref, y_ref, r_ref):
    c = pl.program_id(2)

    @pl.when(c == 0)
    def _():
        r_ref[...] = jnp.zeros_like(r_ref)

    for hd in range(RET_HEADS):
        qs = slice(hd * RET_QK_DIM, (hd + 1) * RET_QK_DIM)
        vs = slice(hd * RET_V_DIM, (hd + 1) * RET_V_DIM)
        q = q_ref[0, :, qs].astype(BF16)
        k = k_ref[0, :, qs]
        v = v_ref[0, :, vs].astype(BF16)
        state = r_ref[hd]
        scores = _dot_nt(q, k.astype(BF16)) * dm_ref[0, hd]
        inner = _dot(scores.astype(BF16), v)
        cross = _dot(q, state.astype(BF16)) * qd_ref[0, hd]
        y_ref[0, 0, :, vs] = inner + cross
        kdec = (k * kd_ref[0, hd]).astype(BF16)
        r_ref[hd] = state * cg_ref[hd] + _dot_tn(kdec, v)


def _ret_scan(q, k, v, dmat, qdec, kdec, cgam):
    b, l, d = q.shape
    Q = RET_CHUNK
    nc = l // Q
    H = RET_HEADS

    def cc(dr, c):
        return c + dr * (nc - 1 - 2 * c)

    return pl.pallas_call(
        _ret_scan_body,
        out_shape=jax.ShapeDtypeStruct((2, b, l, RET_V_TOTAL), F32),
        grid=(b, 2, nc),
        in_specs=[pl.BlockSpec((1, Q, d), lambda i, dr, c: (i, cc(dr, c), 0)),
                  pl.BlockSpec((1, Q, d), lambda i, dr, c: (i, cc(dr, c), 0)),
                  pl.BlockSpec((1, Q, RET_V_TOTAL), lambda i, dr, c: (i, cc(dr, c), 0)),
                  pl.BlockSpec((1, H, Q, Q), lambda i, dr, c: (dr, 0, 0, 0)),
                  pl.BlockSpec((1, H, Q, 1), lambda i, dr, c: (dr, 0, 0, 0)),
                  pl.BlockSpec((1, H, Q, 1), lambda i, dr, c: (dr, 0, 0, 0)),
                  _const_spec((H, 1, RET_V_DIM))],
        out_specs=pl.BlockSpec((1, 1, Q, RET_V_TOTAL), lambda i, dr, c: (dr, i, cc(dr, c), 0)),
        scratch_shapes=[pltpu.VMEM((H, RET_QK_DIM, RET_V_DIM), F32)],
        compiler_params=_params("parallel", "arbitrary", "arbitrary"),
        name="ret_scan",
    )(q, k, v, dmat, qdec, kdec, cgam)


def _ret_out_body(y_ref, gate_ref, x_ref, gg_ref, gb_ref, w_ref, g_ref, o_ref):
    acc = None
    for hd in range(RET_HEADS):
        sl = slice(hd * RET_V_DIM, (hd + 1) * RET_V_DIM)
        y = y_ref[0, 0, :, sl] + y_ref[1, 0, :, sl]
        mu = jnp.mean(y, axis=-1, keepdims=True)
        yc = y - mu
        yn = yc * lax.rsqrt(jnp.mean(yc * yc, axis=-1, keepdims=True) + RET_GN_EPS)
        yn = yn * gg_ref[:, sl] + gb_ref[:, sl]
        gt = gate_ref[0, :, sl]
        part = _dot((gt * _sigmoid(gt) * yn).astype(BF16), w_ref[sl, :])
        acc = part if acc is None else acc + part
    o_ref[0] = x_ref[0] + _rms(acc, g_ref[...], NORM_EPS)


def _ret_out(y, gate, x, gn_g, gn_b, w_out, g):
    b, l, d = x.shape
    tm = _tile(l, 256)
    return pl.pallas_call(
        _ret_out_body,
        out_shape=jax.ShapeDtypeStruct((b, l, d), F32),
        grid=(b, l // tm),
        in_specs=[pl.BlockSpec((2, 1, tm, RET_V_TOTAL), lambda i, j: (0, i, j, 0)),
                  pl.BlockSpec((1, tm, RET_V_TOTAL), lambda i, j: (i, j, 0)),
                  pl.BlockSpec((1, tm, d), lambda i, j: (i, j, 0)),
                  _const_spec((1, RET_V_TOTAL)), _const_spec((1, RET_V_TOTAL)),
                  _resident_spec(w_out.shape),
                  _const_spec((1, d))],
        out_specs=pl.BlockSpec((1, tm, d), lambda i, j: (i, j, 0)),
        compiler_params=_params("parallel", "parallel"),
        name="ret_out",
    )(y, gate, x, gn_g, gn_b, w_out, g)


def _ret_layer(x, ng, p):
    l = x.shape[1]
    q, k, v, gate = _ret_in(x, ng[2:3], p["w_in"], p["cos"][:l], p["sin"][:l])
    y = _ret_scan(q, k, v, p["dmat"], p["qdec"], p["kdec"], p["cgam"])
    return _ret_out(y, gate, x, p["gn_g"], p["gn_b"], p["w_out"], ng[3:4])


def _ret_prepare(j, seq, w_in, gn_g, gn_b, w_out):
    H, dk = RET_HEADS, RET_QK_DIM
    perm = np.concatenate([np.arange(0, dk, 2), np.arange(1, dk, 2)])
    cols = np.concatenate([hd * dk + perm for hd in range(H)])
    w = w_in[j]
    w = jnp.concatenate([w[:, cols], w[:, D_MODEL + cols], w[:, 2 * D_MODEL:]], axis=1).astype(BF16)
    inv = ROPE_BASE ** (-jnp.arange(0, dk, 2, dtype=F32) / dk)
    ang = jnp.arange(seq, dtype=F32)[:, None] * inv[None, :]
    log_gamma = jnp.log1p(-jnp.exp2(-5.0 - jnp.arange(H, dtype=F32)))
    idx = jnp.arange(RET_CHUNK, dtype=F32)
    diff = idx[:, None] - idx[None, :]
    lg = log_gamma[:, None, None]
    dm_f = jnp.where((diff >= 0)[None], jnp.exp(lg * jnp.maximum(diff, 0.0)[None]), 0.0)
    dm_b = jnp.where((diff < 0)[None], jnp.exp(lg * jnp.maximum(-diff, 0.0)[None]), 0.0)
    up = jnp.exp(log_gamma[:, None] * (idx[None, :] + 1.0))
    down = jnp.exp(log_gamma[:, None] * (RET_CHUNK - 1.0 - idx[None, :]))
    qdec = jnp.stack([up, up[:, ::-1]])[..., None]
    kdec = jnp.stack([down, down[:, ::-1]])[..., None]
    cgam = jnp.broadcast_to(jnp.exp(log_gamma * RET_CHUNK)[:, None, None], (H, 1, RET_V_DIM))
    return dict(w_in=w, cos=jnp.cos(ang), sin=jnp.sin(ang), dmat=jnp.stack([dm_f, dm_b]),
                qdec=qdec, kdec=kdec, cgam=cgam,
                gn_g=gn_g[j][None, :], gn_b=gn_b[j][None, :], w_out=w_out[j].astype(BF16))


def _rwkv_in_body(xc_ref, xp_ref, xn_ref, g_ref, mix_ref, wrkv_ref, w0_ref, w1_ref, w2_ref,
                  a0_ref, a1_ref, a2_ref, g1_ref, g2_ref, kk_ref_, ka_ref, rk_ref, ones_ref,
                  r_ref, v_ref, kn_ref, gate_ref, bonus_ref, lw_ref, kd_ref, bb_ref, ext_ref):
    i = pl.program_id(1)
    nt = pl.num_programs(1)
    tm = xc_ref.shape[1]
    g = g_ref[...]
    hc = _rms(xc_ref[0], g, NORM_EPS)
    ext_ref[0:HALO, :] = jnp.where(i > 0, _rms(xp_ref[0], g, NORM_EPS), 0.0)
    ext_ref[HALO:HALO + tm, :] = hc
    ext_ref[HALO + tm:2 * HALO + tm, :] = jnp.where(i < nt - 1, _rms(xn_ref[0], g, NORM_EPS), 0.0)
    xx = 0.5 * (ext_ref[HALO - 1:HALO - 1 + tm, :] + ext_ref[HALO + 1:HALO + 1 + tm, :]) - hc

    def mixed(n):
        return (hc + xx * mix_ref[n:n + 1, :]).astype(BF16)

    ones_bd = ones_ref[...]
    r = _dot(mixed(0), wrkv_ref[0])
    k = _dot(mixed(1), wrkv_ref[1])
    v = _dot(mixed(2), wrkv_ref[2])
    th = jnp.tanh(_dot(mixed(3), w1_ref[...])).astype(BF16)
    aa = _dot(mixed(4), a1_ref[...]).astype(BF16)
    gate_ref[0] = _dot(_sigmoid(_dot(mixed(5), g1_ref[...])).astype(BF16), g2_ref[...])
    kk = k * kk_ref_[...]
    kk = kk / jnp.maximum(jnp.sqrt(_segsum64(kk * kk, ones_bd)), 1e-12)
    r_ref[0] = r
    v_ref[0] = v
    kn_ref[0] = kk
    ksum = None
    for d in range(2):
        wpre = w0_ref[d:d + 1, :] + _dot(th, w2_ref[d])
        lw_ref[d, 0] = -jnp.exp(-_softplus(-wpre) - 0.5)
        a = _sigmoid(a0_ref[d:d + 1, :] + _dot(aa, a2_ref[d]))
        kd = k * (1.0 + (a - 1.0) * ka_ref[...])
        kd_ref[d, 0] = kd
        bb_ref[d, 0] = kk * a
        ksum = kd if ksum is None else ksum + kd
    bonus_ref[0] = _segsum64(r * ksum * rk_ref[...], ones_bd) * v


def _rwkv_in(x, g, p):
    b, l, d = x.shape
    tm = _tile(l, 256)
    prev, nxt = _halo_specs(tm, l, d)
    tok = pl.BlockSpec((1, tm, d), lambda i, j: (i, j, 0))
    tok2 = pl.BlockSpec((2, 1, tm, d), lambda i, j: (0, i, j, 0))
    one = jax.ShapeDtypeStruct((b, l, d), F32)
    two = jax.ShapeDtypeStruct((2, b, l, d), F32)
    names = ["mix", "w_rkv", "w0", "w1", "w2", "a0", "a1", "a2", "g1", "g2", "k_k", "k_a", "r_k", "ones_bd"]
    big = {"w_rkv"}
    specs = [(_resident_spec if n in big else _const_spec)(p[n].shape) for n in names]
    return pl.pallas_call(
        _rwkv_in_body,
        out_shape=(one, one, one, one, one, two, two, two),
        grid=(b, l // tm),
        in_specs=[tok, prev, nxt, _const_spec((1, d))] + specs,
        out_specs=(tok, tok, tok, tok, tok, tok2, tok2, tok2),
        scratch_shapes=[pltpu.VMEM((tm + 2 * HALO, d), F32)],
        compiler_params=_params("parallel", "parallel"),
        name="rwkv_in",
    )(x, x, x, g, *[p[n] for n in names])


RWKV_SOLVE_BASE = 8


def _unit_triangular_solve(n, rhs, eye, blocks):
    def mm(a, b):
        return _dot(a.astype(BF16), b.astype(BF16))

    c = n.shape[0]
    n0 = jnp.where(blocks[0], n, 0.0)
    x = eye + n0
    m = n0
    for _ in range(int(math.log2(RWKV_SOLVE_BASE)) - 1):
        m = mm(m, m)
        x = x + mm(m, x)
    for j in range(1, len(blocks)):
        off = jnp.where(jnp.logical_and(blocks[j], jnp.logical_not(blocks[j - 1])), n, 0.0)
        x = x + mm(mm(x, off), x)
    off = jnp.where(blocks[-1], 0.0, n)
    w = mm(x, rhs)
    return w + mm(x, mm(off, w))


def _rwkv_scan_body(r_ref, v_ref, kn_ref, lw_ref, kd_ref, bb_ref, tm_ref, y_ref, s_ref):
    d = pl.program_id(1)
    c = pl.program_id(2)
    C = RWKV_CHUNK
    N = RWKV_HEAD_DIM

    @pl.when(c == 0)
    def _():
        s_ref[...] = jnp.zeros_like(s_ref)

    tmat = tm_ref[0]
    incl = tmat > 0
    row = lax.broadcasted_iota(jnp.int32, (C, C), 0)
    col = lax.broadcasted_iota(jnp.int32, (C, C), 1)
    strict = jnp.logical_and(incl, row != col)
    eye = jnp.where(row == col, 1.0, 0.0)
    blocks = []
    size = RWKV_SOLVE_BASE
    while size < C:
        shift = int(math.log2(size))
        blocks.append(jnp.right_shift(row, shift) == jnp.right_shift(col, shift))
        size *= 2

    lw = lw_ref[0, 0]
    cum = _dot01_left(tmat, lw)
    total = jnp.where(d == 0, cum[C - 1:C, :], cum[0:1, :])
    w_in = jnp.exp(cum)
    w_inv = jnp.exp(-cum)
    w_out = jnp.exp(total - cum)
    kn = kn_ref[0]
    bb = bb_ref[0, 0]
    kd = kd_ref[0, 0]
    a_hat = -kn * jnp.exp(cum - lw)
    r_til = r_ref[0] * w_in
    b_til = bb * w_inv
    k_til = kd * w_inv
    b_out = bb * w_out
    k_out = kd * w_out
    w_tot = jnp.exp(total)
    v_all = v_ref[0]

    for h in range(RWKV_HEADS):
        sl = slice(h * N, (h + 1) * N)
        state = s_ref[h]
        st16 = state.astype(BF16)
        v = v_all[:, sl]
        ar = jnp.concatenate([a_hat[:, sl], r_til[:, sl]], axis=0).astype(BF16)
        bk = jnp.concatenate([b_til[:, sl], k_til[:, sl]], axis=0).astype(BF16)
        pm = _dot_nt(ar, bk)
        n_ab = jnp.where(strict, pm[0:C, 0:C], 0.0)
        a_ak = jnp.where(strict, pm[0:C, C:2 * C], 0.0)
        a_rb = jnp.where(incl, pm[C:2 * C, 0:C], 0.0)
        a_rk = jnp.where(incl, pm[C:2 * C, C:2 * C], 0.0)
        from_state = _dot_nt(ar, st16)
        v16 = v.astype(BF16)
        rhs = from_state[0:C] + _dot(a_ak.astype(BF16), v16)
        u16 = _unit_triangular_solve(n_ab, rhs, eye, blocks).astype(BF16)
        uv = jnp.concatenate([u16, v16], axis=0)
        y = from_state[C:2 * C] + _dot(jnp.concatenate([a_rb, a_rk], axis=1).astype(BF16), uv)
        y_ref[0, 0, :, sl] = y
        out_keys = jnp.concatenate([b_out[:, sl], k_out[:, sl]], axis=0).astype(BF16)
        s_ref[h] = state * w_tot[:, sl] + _dot_tn(uv, out_keys)


def _rwkv_scan(r, v, kn, lw, kd, bb, tmats):
    b, l, d = r.shape
    C = RWKV_CHUNK
    nc = l // C

    def cc(dr, c):
        return c + dr * (nc - 1 - 2 * c)

    shared = pl.BlockSpec((1, C, d), lambda i, dr, c: (i, cc(dr, c), 0))
    per_dir = pl.BlockSpec((1, 1, C, d), lambda i, dr, c: (dr, i, cc(dr, c), 0))
    return pl.pallas_call(
        _rwkv_scan_body,
        out_shape=jax.ShapeDtypeStruct((2, b, l, d), F32),
        grid=(b, 2, nc),
        in_specs=[shared, shared, shared, per_dir, per_dir, per_dir,
                  pl.BlockSpec((1, C, C), lambda i, dr, c: (dr, 0, 0))],
        out_specs=per_dir,
        scratch_shapes=[pltpu.VMEM((RWKV_HEADS, RWKV_HEAD_DIM, RWKV_HEAD_DIM), F32)],
        compiler_params=_params("parallel", "arbitrary", "arbitrary"),
        name="rwkv_scan",
    )(r, v, kn, lw, kd, bb, tmats)


def _rwkv_out_body(y_ref, bonus_ref, gate_ref, x_ref, lg_ref, lb_ref, ones_ref, w_ref, g_ref, o_ref):
    ones_bd = ones_ref[...]
    y = y_ref[0, 0] + y_ref[1, 0]
    inv_n = 1.0 / RWKV_HEAD_DIM
    yc = y - _segsum64(y, ones_bd) * inv_n
    yn = yc * lax.rsqrt(_segsum64(yc * yc, ones_bd) * inv_n + RWKV_GN_EPS)
    yn = yn * lg_ref[...] + lb_ref[...] + bonus_ref[0]
    out = _dot((yn * gate_ref[0]).astype(BF16), w_ref[...])
    o_ref[0] = x_ref[0] + _rms(out, g_ref[...], NORM_EPS)


def _rwkv_out(y, bonus, gate, x, p, g):
    b, l, d = x.shape
    tm = _tile(l, 256)
    tok = pl.BlockSpec((1, tm, d), lambda i, j: (i, j, 0))
    return pl.pallas_call(
        _rwkv_out_body,
        out_shape=jax.ShapeDtypeStruct((b, l, d), F32),
        grid=(b, l // tm),
        in_specs=[pl.BlockSpec((2, 1, tm, d), lambda i, j: (0, i, j, 0)), tok, tok, tok,
                  _const_spec((1, d)), _const_spec((1, d)), _const_spec(p["ones_bd"].shape),
                  _resident_spec(p["w_out"].shape), _const_spec((1, d))],
        out_specs=tok,
        compiler_params=_params("parallel", "parallel"),
        name="rwkv_out",
    )(y, bonus, gate, x, p["ln_g"], p["ln_b"], p["ones_bd"], p["w_out"], g)


def _rwkv_layer(x, ng, p):
    r, v, kn, gate, bonus, lw, kd, bb = _rwkv_in(x, ng[2:3], p)
    y = _rwkv_scan(r, v, kn, lw, kd, bb, p["tmats"])
    return _rwkv_out(y, bonus, gate, x, p, ng[3:4])


def _rwkv_prepare(j, mix, w_rkv, w0, w1, w2, a0, a1, a2, g1, g2, k_k, k_a, r_k, ln_g, ln_b, w_out):
    def cat_lora_in(w):
        return jnp.concatenate([w[0], w[1]], axis=1).astype(BF16)

    def pad_lora_out(w):
        zero = jnp.zeros_like(w[0])
        return jnp.stack([jnp.concatenate([w[0], zero], axis=0),
                          jnp.concatenate([zero, w[1]], axis=0)]).astype(BF16)

    seg = np.arange(256) // RWKV_HEAD_DIM
    ones_bd = jnp.asarray(seg[:, None] == seg[None, :], dtype=BF16)
    tril = jnp.tril(jnp.ones((RWKV_CHUNK, RWKV_CHUNK), F32))
    return dict(mix=mix[j], w_rkv=w_rkv[j].astype(BF16), w0=w0[j], w1=cat_lora_in(w1[j]), w2=pad_lora_out(w2[j]),
                a0=a0[j], a1=cat_lora_in(a1[j]), a2=pad_lora_out(a2[j]),
                g1=g1[j].astype(BF16), g2=g2[j].astype(BF16),
                k_k=k_k[j][None, :], k_a=k_a[j][None, :], r_k=r_k[j].reshape(1, D_MODEL),
                ln_g=ln_g[j][None, :], ln_b=ln_b[j][None, :], w_out=w_out[j].astype(BF16),
                ones_bd=ones_bd, tmats=jnp.stack([tril, tril.T]).astype(BF16))


def _trunk(x, mem, layers):
    for lay in layers:
        ng = lay["ng"]
        x = _ffn(x, ng[0:2], lay["ffn_in0"], lay["ffn_out0"])
        x = lay["mixer"](x, ng, lay["mixer_params"])
        kv = _memkv(mem, lay["mem_g"], lay["wkv"])
        x = _xattn(x, ng[4:6], kv, lay["wq"], lay["wo"])
        x = _ffn(x, ng[6:8], lay["ffn_in1"], lay["ffn_out1"])
    return x


def kernel(x_prompt, x_sample, mem_prompt, mem_sample, norm_g, mem_norm_g, ffn_w_in, ffn_w_out, xa_wq, xa_wkv, xa_wo, ssd_w_in, ssd_conv_w, ssd_conv_b, ssd_dt_bias, ssd_a_log, ssd_d, ssd_norm_g, ssd_w_out, rwkv_mix, rwkv_w_rkv, rwkv_w0, rwkv_w1, rwkv_w2, rwkv_a0, rwkv_a1, rwkv_a2, rwkv_g1, rwkv_g2, rwkv_k_k, rwkv_k_a, rwkv_r_k, rwkv_ln_g, rwkv_ln_b, rwkv_w_out, ret_w_in, ret_gn_g, ret_gn_b, ret_w_out):
    assert x_prompt.shape[1] == x_sample.shape[1]
    seq = x_prompt.shape[1]
    layers = []
    for i in range(DEPTH):
        kind, j = i % 3, i // 3
        if kind == 0:
            mixer = _ssd_layer
            mp = _ssd_prepare(j, ssd_w_in, ssd_conv_w, ssd_conv_b, ssd_dt_bias, ssd_a_log, ssd_d,
                              ssd_norm_g, ssd_w_out)
        elif kind == 1:
            mixer = _rwkv_layer
            mp = _rwkv_prepare(j, rwkv_mix, rwkv_w_rkv, rwkv_w0, rwkv_w1, rwkv_w2, rwkv_a0, rwkv_a1, rwkv_a2,
                               rwkv_g1, rwkv_g2, rwkv_k_k, rwkv_k_a, rwkv_r_k, rwkv_ln_g, rwkv_ln_b, rwkv_w_out)
        else:
            mixer = _ret_layer
            mp = _ret_prepare(j, seq, ret_w_in, ret_gn_g, ret_gn_b, ret_w_out)
        layers.append(dict(
            ng=norm_g[i], mem_g=mem_norm_g[i][None, :], mixer=mixer, mixer_params=mp,
            ffn_in0=ffn_w_in[i, 0].astype(BF16), ffn_out0=ffn_w_out[i, 0].astype(BF16),
            ffn_in1=ffn_w_in[i, 1].astype(BF16), ffn_out1=ffn_w_out[i, 1].astype(BF16),
            wq=xa_wq[i].astype(BF16), wkv=xa_wkv[i].astype(BF16), wo=xa_wo[i].astype(BF16)))
    return (_trunk(x_prompt, mem_prompt, layers), _trunk(x_sample, mem_sample, layers))
```

```python
import functools
import math

import jax
import jax.numpy as jnp
import numpy as np
from jax import lax
from jax.experimental import pallas as pl
from jax.experimental.pallas import tpu as pltpu

F32 = jnp.float32
BF16 = jnp.bfloat16

D_MODEL = 1024
DEPTH = 4
NORM_EPS = 1e-6
D_FF = 2816
XA_HEADS = 4
XA_HEAD_DIM = D_MODEL // XA_HEADS

SSD_D_INNER = 2 * D_MODEL
SSD_HEAD_DIM = 64
SSD_HEADS = SSD_D_INNER // SSD_HEAD_DIM
SSD_GROUPS = 4
SSD_HPG = SSD_HEADS // SSD_GROUPS
SSD_GW = SSD_HPG * SSD_HEAD_DIM
SSD_STATE = 128
SSD_CONV_K = 5
SSD_CONV_DIM = SSD_D_INNER + 2 * SSD_GROUPS * SSD_STATE
SSD_CHUNK = 128
SSD_NORM_EPS = 1e-5

RWKV_HEAD_DIM = 64
RWKV_HEADS = D_MODEL // RWKV_HEAD_DIM
RWKV_GN_EPS = 64e-5
RWKV_CHUNK = 64

RET_HEADS = 4
RET_QK_DIM = D_MODEL // RET_HEADS
RET_V_DIM = 2 * RET_QK_DIM
RET_V_TOTAL = RET_HEADS * RET_V_DIM
RET_CHUNK = 128
RET_GN_EPS = 1e-6
ROPE_BASE = 10000.0

LANES = 128
HALO = 16
VMEM_LIMIT = 56 * 1024 * 1024


def _rms(x, g, eps):
    return x * lax.rsqrt(jnp.mean(x * x, axis=-1, keepdims=True) + eps) * g


def _sigmoid(x):
    return 1.0 / (1.0 + jnp.exp(-x))


def _softplus(x):
    return jnp.maximum(x, 0.0) + jnp.log1p(jnp.exp(-jnp.abs(x)))


def _dot(a, b):
    return jnp.dot(a, b, preferred_element_type=F32)


def _dot_nt(a, b):
    return lax.dot_general(a, b, (((1,), (1,)), ((), ())), preferred_element_type=F32)


def _dot_tn(a, b):
    return lax.dot_general(a, b, (((0,), (0,)), ((), ())), preferred_element_type=F32)


def _split3(x):
    hi = x.astype(BF16)
    r1 = x - hi.astype(F32)
    mid = r1.astype(BF16)
    lo = (r1 - mid.astype(F32)).astype(BF16)
    return hi, mid, lo


def _dot01_right(x, m01):
    hi, mid, lo = _split3(x)
    return _dot(hi, m01) + _dot(mid, m01) + _dot(lo, m01)


def _dot01_left(m01, x):
    hi, mid, lo = _split3(x)
    return _dot(m01, hi) + _dot(m01, mid) + _dot(m01, lo)


def _segsum64(x, ones_bd):
    hi, mid, lo = _split3(x)
    outs = []
    for j in range(x.shape[-1] // 256):
        sl = slice(j * 256, (j + 1) * 256)
        outs.append(_dot(hi[:, sl], ones_bd) + _dot(mid[:, sl], ones_bd) + _dot(lo[:, sl], ones_bd))
    return jnp.concatenate(outs, axis=-1)


def _const_spec(shape):
    nd = len(shape)
    return pl.BlockSpec(shape, lambda *_: (0,) * nd)


def _resident_spec(shape):
    nd = len(shape)
    return pl.BlockSpec(shape, lambda *_: (0,) * nd, pipeline_mode=pl.Buffered(1))


def _params(*sem):
    return pltpu.CompilerParams(dimension_semantics=sem, vmem_limit_bytes=VMEM_LIMIT)


def _tile(n, want):
    t = min(n, want)
    assert n % t == 0, (n, t)
    return t


def _halo_specs(tm, l, d):
    per = tm // HALO
    last = l // HALO - 1
    prev = pl.BlockSpec((1, HALO, d), lambda b, i: (b, jnp.maximum(i * per - 1, 0), 0))
    nxt = pl.BlockSpec((1, HALO, d), lambda b, i: (b, jnp.minimum((i + 1) * per, last), 0))
    return prev, nxt


FFN_COLS = 256


def _ffn_body(x_ref, g_ref, win_ref, wout_ref, o_ref, acc_ref):
    x = x_ref[...]
    h = _rms(x, g_ref[0:1, :], NORM_EPS).astype(BF16)
    for c in range(D_FF // FFN_COLS):
        lo, hi = c * FFN_COLS, (c + 1) * FFN_COLS
        gate = _dot(h, win_ref[:, lo:hi])
        up = _dot(h, win_ref[:, D_FF + lo:D_FF + hi])
        act = (gate * _sigmoid(gate) * up).astype(BF16)
        part = _dot(act, wout_ref[lo:hi, :])
        if c == 0:
            acc_ref[...] = part
        else:
            acc_ref[...] += part
    o_ref[...] = x + 0.5 * _rms(acc_ref[...], g_ref[1:2, :], NORM_EPS)


def _ffn(x, g2, w_in, w_out):
    b, l, d = x.shape
    t = b * l
    tm = _tile(t, 512)
    out = pl.pallas_call(
        _ffn_body,
        out_shape=jax.ShapeDtypeStruct((t, d), F32),
        grid=(t // tm,),
        in_specs=[pl.BlockSpec((tm, d), lambda i: (i, 0)),
                  _const_spec((2, d)),
                  _resident_spec(w_in.shape),
                  _resident_spec(w_out.shape)],
        out_specs=pl.BlockSpec((tm, d), lambda i: (i, 0)),
        scratch_shapes=[pltpu.VMEM((tm, d), F32)],
        compiler_params=_params("parallel"),
        name="ffn",
    )(x.reshape(t, d), g2, w_in, w_out)
    return out.reshape(b, l, d)


def _memkv_body(m_ref, g_ref, w_ref, o_ref):
    h = _rms(m_ref[...], g_ref[...], NORM_EPS).astype(BF16)
    o_ref[...] = _dot(h, w_ref[...]).astype(BF16)


def _memkv(mem, g, wkv):
    b, m, d = mem.shape
    t = b * m
    tm = _tile(t, 256)
    out = pl.pallas_call(
        _memkv_body,
        out_shape=jax.ShapeDtypeStruct((t, 2 * d), BF16),
        grid=(t // tm,),
        in_specs=[pl.BlockSpec((tm, d), lambda i: (i, 0)),
                  _const_spec((1, d)),
                  _resident_spec(wkv.shape)],
        out_specs=pl.BlockSpec((tm, 2 * d), lambda i: (i, 0)),
        compiler_params=_params("parallel"),
        name="memkv",
    )(mem.reshape(t, d), g, wkv)
    return out.reshape(b, m, 2 * d)


def _xattn_body(x_ref, g_ref, kv_ref, wq_ref, wo_ref, o_ref):
    x = x_ref[0]
    h = _rms(x, g_ref[0:1, :], NORM_EPS).astype(BF16)
    q = _dot(h, wq_ref[...])
    scale = XA_HEAD_DIM ** -0.5
    outs = []
    for hd in range(XA_HEADS):
        lo, hi = hd * XA_HEAD_DIM, (hd + 1) * XA_HEAD_DIM
        s = _dot_nt(q[:, lo:hi].astype(BF16), kv_ref[0, :, lo:hi]) * scale
        s = s - jnp.max(s, axis=-1, keepdims=True)
        p = jnp.exp(s)
        p = p / jnp.sum(p, axis=-1, keepdims=True)
        outs.append(_dot(p.astype(BF16), kv_ref[0, :, D_MODEL + lo:D_MODEL + hi]))
    o = jnp.concatenate(outs, axis=-1).astype(BF16)
    y = _dot(o, wo_ref[...])
    o_ref[0] = x + _rms(y, g_ref[1:2, :], NORM_EPS)


def _xattn(x, g2, kv, wq, wo):
    b, l, d = x.shape
    m = kv.shape[1]
    tm = _tile(l, 512)
    return pl.pallas_call(
        _xattn_body,
        out_shape=jax.ShapeDtypeStruct((b, l, d), F32),
        grid=(b, l // tm),
        in_specs=[pl.BlockSpec((1, tm, d), lambda i, j: (i, j, 0)),
                  _const_spec((2, d)),
                  pl.BlockSpec((1, m, 2 * d), lambda i, j: (i, 0, 0)),
                  _resident_spec(wq.shape),
                  _resident_spec(wo.shape)],
        out_specs=pl.BlockSpec((1, tm, d), lambda i, j: (i, j, 0)),
        compiler_params=_params("parallel", "parallel"),
        name="xattn",
    )(x, g2, kv, wq, wo)


def _ssd_in_body(xc_ref, xp_ref, xn_ref, g_ref, wz_ref, wx_ref, wdt_ref, cw_ref, cb_ref, dtb_ref,
                 z_ref, xs_ref, bm_ref, cm_ref, dt_ref, ext_ref):
    i = pl.program_id(1)
    nt = pl.num_programs(1)
    tm = xc_ref.shape[1]
    g = g_ref[...]
    hc = _rms(xc_ref[0], g, NORM_EPS).astype(BF16)
    hp = _rms(xp_ref[0], g, NORM_EPS).astype(BF16)
    hn = _rms(xn_ref[0], g, NORM_EPS).astype(BF16)
    z_ref[0] = _dot(hc, wz_ref[...])
    dt_ref[0] = _softplus(_dot(hc, wdt_ref[...]) + dtb_ref[...])
    ext_ref[0:HALO, :] = jnp.where(i > 0, _dot(hp, wx_ref[...]), 0.0)
    ext_ref[HALO:HALO + tm, :] = _dot(hc, wx_ref[...])
    ext_ref[HALO + tm:2 * HALO + tm, :] = jnp.where(i < nt - 1, _dot(hn, wx_ref[...]), 0.0)
    acc = jnp.broadcast_to(cb_ref[...], (tm, SSD_CONV_DIM))
    for j in range(SSD_CONV_K):
        off = HALO - SSD_CONV_K // 2 + j
        acc = acc + cw_ref[j:j + 1, :] * ext_ref[off:off + tm, :]
    y = acc * _sigmoid(acc)
    for gi in range(SSD_GROUPS):
        xs_ref[0, gi] = y[:, gi * SSD_GW:(gi + 1) * SSD_GW]
        bm_ref[0, gi] = y[:, SSD_D_INNER + gi * SSD_STATE:SSD_D_INNER + (gi + 1) * SSD_STATE]
        cm_ref[0, gi] = y[:, SSD_D_INNER + (SSD_GROUPS + gi) * SSD_STATE:
                          SSD_D_INNER + (SSD_GROUPS + gi + 1) * SSD_STATE]


def _ssd_in(x, g, wz, wx, wdt, conv_w, conv_b, dt_bias):
    b, l, d = x.shape
    tm = _tile(l, 256)
    prev, nxt = _halo_specs(tm, l, d)
    G = SSD_GROUPS
    return pl.pallas_call(
        _ssd_in_body,
        out_shape=(jax.ShapeDtypeStruct((b, l, SSD_D_INNER), F32),
                   jax.ShapeDtypeStruct((b, G, l, SSD_GW), F32),
                   jax.ShapeDtypeStruct((b, G, l, SSD_STATE), F32),
                   jax.ShapeDtypeStruct((b, G, l, SSD_STATE), F32),
                   jax.ShapeDtypeStruct((b, l, LANES), F32)),
        grid=(b, l // tm),
        in_specs=[pl.BlockSpec((1, tm, d), lambda i, j: (i, j, 0)), prev, nxt,
                  _const_spec((1, d)),
                  _resident_spec(wz.shape), _resident_spec(wx.shape), _resident_spec(wdt.shape),
                  _const_spec(conv_w.shape), _const_spec(conv_b.shape), _const_spec(dt_bias.shape)],
        out_specs=(pl.BlockSpec((1, tm, SSD_D_INNER), lambda i, j: (i, j, 0)),
                   pl.BlockSpec((1, G, tm, SSD_GW), lambda i, j: (i, 0, j, 0)),
                   pl.BlockSpec((1, G, tm, SSD_STATE), lambda i, j: (i, 0, j, 0)),
                   pl.BlockSpec((1, G, tm, SSD_STATE), lambda i, j: (i, 0, j, 0)),
                   pl.BlockSpec((1, tm, LANES), lambda i, j: (i, j, 0))),
        scratch_shapes=[pltpu.VMEM((tm + 2 * HALO, SSD_CONV_DIM), F32)],
        compiler_params=_params("parallel", "parallel"),
        name="ssd_in",
    )(x, x, x, g, wz, wx, wdt, conv_w, conv_b, dt_bias)


def _ssd_scan_body(xsf_ref, bmf_ref, cmf_ref, dtf_ref, xsb_ref, bmb_ref, cmb_ref, dtb_ref,
                   alog_ref, dskip_ref, tm_ref, yf_ref, yb_ref, h_ref):
    c = pl.program_id(1)
    Q = SSD_CHUNK
    P = SSD_HEAD_DIM
    G = SSD_GROUPS
    PAIRS = SSD_HPG // 2

    @pl.when(c == 0)
    def _():
        h_ref[...] = jnp.zeros_like(h_ref)

    lane = lax.broadcasted_iota(jnp.int32, (Q, 2 * P), 1)
    low = lane < P
    refs = [(xsf_ref, bmf_ref, cmf_ref, dtf_ref, yf_ref), (xsb_ref, bmb_ref, cmb_ref, dtb_ref, yb_ref)]
    neg_a = -jnp.exp(alog_ref[...])
    dts, cums, cum_rows, masks = [], [], [], []
    for d in range(2):
        tmat = tm_ref[d]
        dt = refs[d][3][0]
        a = dt * neg_a
        dts.append(dt)
        cums.append(_dot01_left(tmat, a))
        cum_rows.append(_dot01_right(a.T, tm_ref[1 - d]))
        masks.append(tmat > 0)

    groups = [(d, g) for d in range(2) for g in range(G)]
    units = [(d, g, pr) for d, g in groups for pr in range(PAIRS)]
    cm16 = {(d, g): refs[d][2][0, g].astype(BF16) for d, g in groups}
    cb = {(d, g): _dot_nt(cm16[d, g], refs[d][1][0, g].astype(BF16)) for d, g in groups}
    hst = {(d, g): h_ref[d, g] for d, g in groups}
    yoff = {(d, g): _dot(cm16[d, g], hst[d, g].astype(BF16)) for d, g in groups}

    def head_col(d, g, r):
        return d * SSD_HEADS + g * SSD_HPG + r

    def pair_of(arr, k0, k1):
        return jnp.where(low, jnp.broadcast_to(arr[:, k0:k0 + 1], (Q, 2 * P)),
                         jnp.broadcast_to(arr[:, k1:k1 + 1], (Q, 2 * P)))

    lhs, rhs, xdt_l, cum_pair_l, xs_l = [], [], [], [], []
    for d, g, pr in units:
        k0, k1 = head_col(d, g, 2 * pr), head_col(d, g, 2 * pr + 1)
        sl = slice(pr * 2 * P, (pr + 1) * 2 * P)
        bc0 = jnp.broadcast_to(cums[d][:, k0:k0 + 1], (Q, Q))
        bc1 = jnp.broadcast_to(cums[d][:, k1:k1 + 1], (Q, Q))
        m0 = cb[d, g] * jnp.exp(jnp.where(masks[d], bc0 - cum_rows[d][k0:k0 + 1, :], -jnp.inf))
        m1 = cb[d, g] * jnp.exp(jnp.where(masks[d], bc1 - cum_rows[d][k1:k1 + 1, :], -jnp.inf))
        lhs.append(jnp.concatenate([m0, m1], axis=1).astype(BF16))
        cum_pair_l.append(pair_of(cums[d], k0, k1))
        xs_pair = refs[d][0][0, g, :, sl]
        xdt = xs_pair * pair_of(dts[d], k0, k1)
        xs_l.append(xs_pair)
        xdt_l.append(xdt)
        rhs.append(jnp.concatenate([jnp.where(low, xdt, 0.0), jnp.where(low, 0.0, xdt)], axis=0).astype(BF16))
    ydiag = [_dot(a, b) for a, b in zip(lhs, rhs)]
    xw, keep = {}, {}
    for i, (d, g, pr) in enumerate(units):
        sl = slice(pr * 2 * P, (pr + 1) * 2 * P)
        cum_pair = cum_pair_l[i]
        tot_pair = cum_pair[Q - 1:Q, :] if d == 0 else cum_pair[0:1, :]
        y_pair = ydiag[i] + yoff[d, g][:, sl] * jnp.exp(cum_pair)
        if d == 0:
            y_pair = y_pair + xs_l[i] * dskip_ref[g, :, sl]
        refs[d][4][0, g, :, sl] = y_pair
        xw[d, g, pr] = (xdt_l[i] * jnp.exp(tot_pair - cum_pair)).astype(BF16)
        keep[d, g, pr] = jnp.exp(tot_pair)
    for d, g in groups:
        xw_g = jnp.concatenate([xw[d, g, pr] for pr in range(PAIRS)], axis=1)
        keep_g = jnp.concatenate([keep[d, g, pr] for pr in range(PAIRS)], axis=1)
        h_ref[d, g] = hst[d, g] * keep_g + _dot(refs[d][1][0, g].T.astype(BF16), xw_g)


def _ssd_scan(xs, bm, cm, dt, alog, dskip, tmats):
    b, G, l, _ = xs.shape
    Q = SSD_CHUNK
    nc = l // Q

    def grouped(width, back):
        return pl.BlockSpec((1, G, Q, width), lambda i, c: (i, 0, nc - 1 - c if back else c, 0))

    def flat(back):
        return pl.BlockSpec((1, Q, LANES), lambda i, c: (i, nc - 1 - c if back else c, 0))

    out = jax.ShapeDtypeStruct((b, G, l, SSD_GW), F32)
    return pl.pallas_call(
        _ssd_scan_body,
        out_shape=(out, out),
        grid=(b, nc),
        in_specs=[grouped(SSD_GW, False), grouped(SSD_STATE, False), grouped(SSD_STATE, False), flat(False),
                  grouped(SSD_GW, True), grouped(SSD_STATE, True), grouped(SSD_STATE, True), flat(True),
                  _const_spec((1, LANES)), _const_spec((G, 1, SSD_GW)), _const_spec((2, Q, Q))],
        out_specs=(grouped(SSD_GW, False), grouped(SSD_GW, True)),
        scratch_shapes=[pltpu.VMEM((2, G, SSD_STATE, SSD_GW), F32)],
        compiler_params=_params("parallel", "arbitrary"),
        name="ssd_scan",
    )(xs, bm, cm, dt, xs, bm, cm, dt, alog, dskip, tmats)


def _ssd_out_body(yf_ref, yb_ref, z_ref, x_ref, ng_ref, w_ref, g_ref, o_ref):
    acc = None
    for gi in range(SSD_GROUPS):
        sl = slice(gi * SSD_GW, (gi + 1) * SSD_GW)
        z = z_ref[0, :, sl]
        y = (yf_ref[0, gi] + yb_ref[0, gi]) * (z * _sigmoid(z))
        yn = _rms(y, ng_ref[:, sl], SSD_NORM_EPS).astype(BF16)
        part = _dot(yn, w_ref[sl, :])
        acc = part if acc is None else acc + part
    o_ref[0] = x_ref[0] + _rms(acc, g_ref[...], NORM_EPS)


def _ssd_out(yf, yb, z, x, norm_g, w_out, g):
    b, l, d = x.shape
    tm = _tile(l, 256)
    G = SSD_GROUPS
    return pl.pallas_call(
        _ssd_out_body,
        out_shape=jax.ShapeDtypeStruct((b, l, d), F32),
        grid=(b, l // tm),
        in_specs=[pl.BlockSpec((1, G, tm, SSD_GW), lambda i, j: (i, 0, j, 0)),
                  pl.BlockSpec((1, G, tm, SSD_GW), lambda i, j: (i, 0, j, 0)),
                  pl.BlockSpec((1, tm, SSD_D_INNER), lambda i, j: (i, j, 0)),
                  pl.BlockSpec((1, tm, d), lambda i, j: (i, j, 0)),
                  _const_spec((1, SSD_D_INNER)),
                  _resident_spec(w_out.shape),
                  _const_spec((1, d))],
        out_specs=pl.BlockSpec((1, tm, d), lambda i, j: (i, j, 0)),
        compiler_params=_params("parallel", "parallel"),
        name="ssd_out",
    )(yf, yb, z, x, norm_g, w_out, g)


def _ssd_layer(x, ng, p):
    z, xs, bm, cm, dt = _ssd_in(x, ng[2:3], p["wz"], p["wx"], p["wdt"], p["conv_w"], p["conv_b"], p["dt_bias"])
    yf, yb = _ssd_scan(xs, bm, cm, dt, p["alog"], p["dskip"], p["tmats"])
    return _ssd_out(yf, yb, z, x, p["norm_g"], p["w_out"], ng[3:4])


def _ssd_prepare(j, w_in, conv_w, conv_b, dt_bias, a_log, d_skip, norm_g, w_out):
    G, R = SSD_GROUPS, SSD_HPG
    w = w_in[j]
    wz = w[:, :SSD_D_INNER].astype(BF16)
    wx = w[:, SSD_D_INNER:SSD_D_INNER + SSD_CONV_DIM].astype(BF16)
    pad = LANES - 2 * SSD_HEADS
    wdt = jnp.pad(w[:, SSD_D_INNER + SSD_CONV_DIM:], ((0, 0), (0, pad))).astype(BF16)

    def per_head(v):
        return jnp.pad(v.reshape(1, 2 * SSD_HEADS), ((0, 0), (0, pad)))

    dskip = jnp.repeat(d_skip[j].reshape(G, 1, R), SSD_HEAD_DIM, axis=-1)
    tril = jnp.tril(jnp.ones((SSD_CHUNK, SSD_CHUNK), F32))
    return dict(wz=wz, wx=wx, wdt=wdt, conv_w=conv_w[j], conv_b=conv_b[j][None, :],
                dt_bias=per_head(dt_bias[j]),
                alog=per_head(a_log[j]), dskip=dskip,
                tmats=jnp.stack([tril, tril.T]).astype(BF16),
                norm_g=norm_g[j][None, :], w_out=w_out[j].astype(BF16))


def _ret_in_body(x_ref, g_ref, w_ref, cos_ref, sin_ref, q_ref, k_ref, v_ref, gate_ref):
    h = _rms(x_ref[0], g_ref[...], NORM_EPS).astype(BF16)
    cos = cos_ref[...]
    sin = sin_ref[...]
    half = RET_QK_DIM // 2

    def rot(t, scale):
        outs = []
        for hd in range(RET_HEADS):
            x1 = t[:, hd * RET_QK_DIM:hd * RET_QK_DIM + half]
            x2 = t[:, hd * RET_QK_DIM + half:(hd + 1) * RET_QK_DIM]
            outs.append((x1 * cos - x2 * sin) * scale)
            outs.append((x1 * sin + x2 * cos) * scale)
        return jnp.concatenate(outs, axis=-1)

    q_ref[0] = rot(_dot(h, w_ref[:, 0:D_MODEL]), RET_QK_DIM ** -0.5)
    k_ref[0] = rot(_dot(h, w_ref[:, D_MODEL:2 * D_MODEL]), 1.0)
    v_ref[0] = _dot(h, w_ref[:, 2 * D_MODEL:2 * D_MODEL + RET_V_TOTAL])
    gate_ref[0] = _dot(h, w_ref[:, 2 * D_MODEL + RET_V_TOTAL:])


def _ret_in(x, g, w, cos, sin):
    b, l, d = x.shape
    tm = _tile(l, 256)
    half = RET_QK_DIM // 2
    return pl.pallas_call(
        _ret_in_body,
        out_shape=(jax.ShapeDtypeStruct((b, l, d), F32), jax.ShapeDtypeStruct((b, l, d), F32),
                   jax.ShapeDtypeStruct((b, l, RET_V_TOTAL), F32), jax.ShapeDtypeStruct((b, l, RET_V_TOTAL), F32)),
        grid=(b, l // tm),
        in_specs=[pl.BlockSpec((1, tm, d), lambda i, j: (i, j, 0)),
                  _const_spec((1, d)),
                  _resident_spec(w.shape),
                  pl.BlockSpec((tm, half), lambda i, j: (j, 0)),
                  pl.BlockSpec((tm, half), lambda i, j: (j, 0))],
        out_specs=(pl.BlockSpec((1, tm, d), lambda i, j: (i, j, 0)),
                   pl.BlockSpec((1, tm, d), lambda i, j: (i, j, 0)),
                   pl.BlockSpec((1, tm, RET_V_TOTAL), lambda i, j: (i, j, 0)),
                   pl.BlockSpec((1, tm, RET_V_TOTAL), lambda i, j: (i, j, 0))),
        compiler_params=_params("parallel", "parallel"),
        name="ret_in",
    )(x, g, w, cos, sin)


def _ret_scan_body(q_ref, k_ref, v_ref, dm_ref, qd_ref, kd_ref, cg_ref, y_ref, r_ref):
    c = pl.program_id(2)

    @pl.when(c == 0)
    def _():
        r_ref[...] = jnp.zeros_like(r_ref)

    for hd in range(RET_HEADS):
        qs = slice(hd * RET_QK_DIM, (hd + 1) * RET_QK_DIM)
        vs = slice(hd * RET_V_DIM, (hd + 1) * RET_V_DIM)
        q = q_ref[0, :, qs].astype(BF16)
        k = k_ref[0, :, qs]
        v = v_ref[0, :, vs].astype(BF16)
        state = r_ref[hd]
        scores = _dot_nt(q, k.astype(BF16)) * dm_ref[0, hd]
        inner = _dot(scores.astype(BF16), v)
        cross = _dot(q, state.astype(BF16)) * qd_ref[0, hd]
        y_ref[0, 0, :, vs] = inner + cross
        kdec = (k * kd_ref[0, hd]).astype(BF16)
        r_ref[hd] = state * cg_ref[hd] + _dot_tn(kdec, v)


def _ret_scan(q, k, v, dmat, qdec, kdec, cgam):
    b, l, d = q.shape
    Q = RET_CHUNK
    nc = l // Q
    H = RET_HEADS

    def cc(dr, c):
        return c + dr * (nc - 1 - 2 * c)

    return pl.pallas_call(
        _ret_scan_body,
        out_shape=jax.ShapeDtypeStruct((2, b, l, RET_V_TOTAL), F32),
        grid=(b, 2, nc),
        in_specs=[pl.BlockSpec((1, Q, d), lambda i, dr, c: (i, cc(dr, c), 0)),
                  pl.BlockSpec((1, Q, d), lambda i, dr, c: (i, cc(dr, c), 0)),
                  pl.BlockSpec((1, Q, RET_V_TOTAL), lambda i, dr, c: (i, cc(dr, c), 0)),
                  pl.BlockSpec((1, H, Q, Q), lambda i, dr, c: (dr, 0, 0, 0)),
                  pl.BlockSpec((1, H, Q, 1), lambda i, dr, c: (dr, 0, 0, 0)),
                  pl.BlockSpec((1, H, Q, 1), lambda i, dr, c: (dr, 0, 0, 0)),
                  _const_spec((H, 1, RET_V_DIM))],
        out_specs=pl.BlockSpec((1, 1, Q, RET_V_TOTAL), lambda i, dr, c: (dr, i, cc(dr, c), 0)),
        scratch_shapes=[pltpu.VMEM((H, RET_QK_DIM, RET_V_DIM), F32)],
        compiler_params=_params("parallel", "arbitrary", "arbitrary"),
        name="ret_scan",
    )(q, k, v, dmat, qdec, kdec, cgam)


def _ret_out_body(y_ref, gate_ref, x_ref, gg_ref, gb_ref, w_ref, g_ref, o_ref):
    acc = None
    for hd in range(RET_HEADS):
        sl = slice(hd * RET_V_DIM, (hd + 1) * RET_V_DIM)
        y = y_ref[0, 0, :, sl] + y_ref[1, 0, :, sl]
        mu = jnp.mean(y, axis=-1, keepdims=True)
        yc = y - mu
        yn = yc * lax.rsqrt(jnp.mean(yc * yc, axis=-1, keepdims=True) + RET_GN_EPS)
        yn = yn * gg_ref[:, sl] + gb_ref[:, sl]
        gt = gate_ref[0, :, sl]
        part = _dot((gt * _sigmoid(gt) * yn).astype(BF16), w_ref[sl, :])
        acc = part if acc is None else acc + part
    o_ref[0] = x_ref[0] + _rms(acc, g_ref[...], NORM_EPS)


def _ret_out(y, gate, x, gn_g, gn_b, w_out, g):
    b, l, d = x.shape
    tm = _tile(l, 256)
    return pl.pallas_call(
        _ret_out_body,
        out_shape=jax.ShapeDtypeStruct((b, l, d), F32),
        grid=(b, l // tm),
        in_specs=[pl.BlockSpec((2, 1, tm, RET_V_TOTAL), lambda i, j: (0, i, j, 0)),
                  pl.BlockSpec((1, tm, RET_V_TOTAL), lambda i, j: (i, j, 0)),
                  pl.BlockSpec((1, tm, d), lambda i, j: (i, j, 0)),
                  _const_spec((1, RET_V_TOTAL)), _const_spec((1, RET_V_TOTAL)),
                  _resident_spec(w_out.shape),
                  _const_spec((1, d))],
        out_specs=pl.BlockSpec((1, tm, d), lambda i, j: (i, j, 0)),
        compiler_params=_params("parallel", "parallel"),
        name="ret_out",
    )(y, gate, x, gn_g, gn_b, w_out, g)


def _ret_layer(x, ng, p):
    l = x.shape[1]
    q, k, v, gate = _ret_in(x, ng[2:3], p["w_in"], p["cos"][:l], p["sin"][:l])
    y = _ret_scan(q, k, v, p["dmat"], p["qdec"], p["kdec"], p["cgam"])
    return _ret_out(y, gate, x, p["gn_g"], p["gn_b"], p["w_out"], ng[3:4])


def _ret_prepare(j, seq, w_in, gn_g, gn_b, w_out):
    H, dk = RET_HEADS, RET_QK_DIM
    perm = np.concatenate([np.arange(0, dk, 2), np.arange(1, dk, 2)])
    cols = np.concatenate([hd * dk + perm for hd in range(H)])
    w = w_in[j]
    w = jnp.concatenate([w[:, cols], w[:, D_MODEL + cols], w[:, 2 * D_MODEL:]], axis=1).astype(BF16)
    inv = ROPE_BASE ** (-jnp.arange(0, dk, 2, dtype=F32) / dk)
    ang = jnp.arange(seq, dtype=F32)[:, None] * inv[None, :]
    log_gamma = jnp.log1p(-jnp.exp2(-5.0 - jnp.arange(H, dtype=F32)))
    idx = jnp.arange(RET_CHUNK, dtype=F32)
    diff = idx[:, None] - idx[None, :]
    lg = log_gamma[:, None, None]
    dm_f = jnp.where((diff >= 0)[None], jnp.exp(lg * jnp.maximum(diff, 0.0)[None]), 0.0)
    dm_b = jnp.where((diff < 0)[None], jnp.exp(lg * jnp.maximum(-diff, 0.0)[None]), 0.0)
    up = jnp.exp(log_gamma[:, None] * (idx[None, :] + 1.0))
    down = jnp.exp(log_gamma[:, None] * (RET_CHUNK - 1.0 - idx[None, :]))
    qdec = jnp.stack([up, up[:, ::-1]])[..., None]
    kdec = jnp.stack([down, down[:, ::-1]])[..., None]
    cgam = jnp.broadcast_to(jnp.exp(log_gamma * RET_CHUNK)[:, None, None], (H, 1, RET_V_DIM))
    return dict(w_in=w, cos=jnp.cos(ang), sin=jnp.sin(ang), dmat=jnp.stack([dm_f, dm_b]),
                qdec=qdec, kdec=kdec, cgam=cgam,
                gn_g=gn_g[j][None, :], gn_b=gn_b[j][None, :], w_out=w_out[j].astype(BF16))


def _rwkv_in_body(xc_ref, xp_ref, xn_ref, g_ref, mix_ref, wrkv_ref, w0_ref, w1_ref, w2_ref,
                  a0_ref, a1_ref, a2_ref, g1_ref, g2_ref, kk_ref_, ka_ref, rk_ref, ones_ref,
                  r_ref, v_ref, kn_ref, gate_ref, bonus_ref, lw_ref, kd_ref, bb_ref, ext_ref):
    i = pl.program_id(1)
    nt = pl.num_programs(1)
    tm = xc_ref.shape[1]
    g = g_ref[...]
    hc = _rms(xc_ref[0], g, NORM_EPS)
    ext_ref[0:HALO, :] = jnp.where(i > 0, _rms(xp_ref[0], g, NORM_EPS), 0.0)
    ext_ref[HALO:HALO + tm, :] = hc
    ext_ref[HALO + tm:2 * HALO + tm, :] = jnp.where(i < nt - 1, _rms(xn_ref[0], g, NORM_EPS), 0.0)
    xx = 0.5 * (ext_ref[HALO - 1:HALO - 1 + tm, :] + ext_ref[HALO + 1:HALO + 1 + tm, :]) - hc

    def mixed(n):
        return (hc + xx * mix_ref[n:n + 1, :]).astype(BF16)

    ones_bd = ones_ref[...]
    r = _dot(mixed(0), wrkv_ref[0])
    k = _dot(mixed(1), wrkv_ref[1])
    v = _dot(mixed(2), wrkv_ref[2])
    th = jnp.tanh(_dot(mixed(3), w1_ref[...])).astype(BF16)
    aa = _dot(mixed(4), a1_ref[...]).astype(BF16)
    gate_ref[0] = _dot(_sigmoid(_dot(mixed(5), g1_ref[...])).astype(BF16), g2_ref[...])
    kk = k * kk_ref_[...]
    kk = kk / jnp.maximum(jnp.sqrt(_segsum64(kk * kk, ones_bd)), 1e-12)
    r_ref[0] = r
    v_ref[0] = v
    kn_ref[0] = kk
    ksum = None
    for d in range(2):
        wpre = w0_ref[d:d + 1, :] + _dot(th, w2_ref[d])
        lw_ref[d, 0] = -jnp.exp(-_softplus(-wpre) - 0.5)
        a = _sigmoid(a0_ref[d:d + 1, :] + _dot(aa, a2_ref[d]))
        kd = k * (1.0 + (a - 1.0) * ka_ref[...])
        kd_ref[d, 0] = kd
        bb_ref[d, 0] = kk * a
        ksum = kd if ksum is None else ksum + kd
    bonus_ref[0] = _segsum64(r * ksum * rk_ref[...], ones_bd) * v


def _rwkv_in(x, g, p):
    b, l, d = x.shape
    tm = _tile(l, 256)
    prev, nxt = _halo_specs(tm, l, d)
    tok = pl.BlockSpec((1, tm, d), lambda i, j: (i, j, 0))
    tok2 = pl.BlockSpec((2, 1, tm, d), lambda i, j: (0, i, j, 0))
    one = jax.ShapeDtypeStruct((b, l, d), F32)
    two = jax.ShapeDtypeStruct((2, b, l, d), F32)
    names = ["mix", "w_rkv", "w0", "w1", "w2", "a0", "a1", "a2", "g1", "g2", "k_k", "k_a", "r_k", "ones_bd"]
    big = {"w_rkv"}
    specs = [(_resident_spec if n in big else _const_spec)(p[n].shape) for n in names]
    return pl.pallas_call(
        _rwkv_in_body,
        out_shape=(one, one, one, one, one, two, two, two),
        grid=(b, l // tm),
        in_specs=[tok, prev, nxt, _const_spec((1, d))] + specs,
        out_specs=(tok, tok, tok, tok, tok, tok2, tok2, tok2),
        scratch_shapes=[pltpu.VMEM((tm + 2 * HALO, d), F32)],
        compiler_params=_params("parallel", "parallel"),
        name="rwkv_in",
    )(x, x, x, g, *[p[n] for n in names])


RWKV_SOLVE_BASE = 8


def _pair_diag(y16, low):
    zero = jnp.zeros_like(y16)
    return jnp.concatenate([jnp.where(low, y16, zero), jnp.where(low, zero, y16)], axis=0)


def _pair_mm(a, b, low):
    return [_dot(x.astype(BF16), _pair_diag(y.astype(BF16), low)) for x, y in zip(a, b)]


def _unit_triangular_solve(n, rhs, eye, blocks, low):
    def add(a, b):
        return [p + q for p, q in zip(a, b)]

    n0 = [jnp.where(blocks[0], t, 0.0) for t in n]
    x = [eye + t for t in n0]
    m = n0
    for _ in range(int(math.log2(RWKV_SOLVE_BASE)) - 1):
        m = _pair_mm(m, m, low)
        x = add(x, _pair_mm(m, x, low))
    for j in range(1, len(blocks)):
        ring = jnp.logical_and(blocks[j], jnp.logical_not(blocks[j - 1]))
        off = [jnp.where(ring, t, 0.0) for t in n]
        x = add(x, _pair_mm(_pair_mm(x, off, low), x, low))
    off = [jnp.where(blocks[-1], 0.0, t) for t in n]
    w = _pair_mm(x, rhs, low)
    return add(w, _pair_mm(x, _pair_mm(off, w, low), low))


RWKV_PAIR = 2 * RWKV_HEAD_DIM
RWKV_PAIRS = RWKV_HEADS // 2


def _rwkv_chunk_operands(d, r, kn, lw, kd, bb, tmat):
    C = RWKV_CHUNK
    cum = _dot01_left(tmat, lw)
    total = cum[C - 1:C, :] if d == 0 else cum[0:1, :]
    w_inv = jnp.exp(-cum)
    w_out = jnp.exp(total - cum)
    return dict(a_hat=-kn * jnp.exp(cum - lw),
                r_til=r * jnp.exp(cum), b_til=bb * w_inv, k_til=kd * w_inv,
                b_out=bb * w_out, k_out=kd * w_out, w_tot=jnp.exp(total))


def _rwkv_scan_body(rf_ref, vf_ref, knf_ref, rb_ref, vb_ref, knb_ref,
                    lwf_ref, kdf_ref, bbf_ref, lwb_ref, kdb_ref, bbb_ref, tm_ref,
                    yf_ref, yb_ref, s_ref):
    c = pl.program_id(1)
    C, W = RWKV_CHUNK, RWKV_PAIR

    @pl.when(c == 0)
    def _():
        s_ref[...] = jnp.zeros_like(s_ref)

    row = lax.broadcasted_iota(jnp.int32, (C, W), 0)
    lane = lax.broadcasted_iota(jnp.int32, (C, W), 1)
    col = jnp.bitwise_and(lane, RWKV_HEAD_DIM - 1)
    low = lane < RWKV_HEAD_DIM
    eye = jnp.where(row == col, 1.0, 0.0)
    blocks = []
    size = RWKV_SOLVE_BASE
    while size < C:
        shift = int(math.log2(size))
        blocks.append(jnp.right_shift(row, shift) == jnp.right_shift(col, shift))
        size *= 2
    incl = [col <= row, col >= row]
    strict = [col < row, col > row]
    incl2 = [jnp.concatenate([m, m], axis=1) for m in incl]
    srow = lax.broadcasted_iota(jnp.int32, (W, W), 0) < RWKV_HEAD_DIM
    scol = lax.broadcasted_iota(jnp.int32, (W, W), 1) < RWKV_HEAD_DIM
    same_head = srow == scol

    ops = [_rwkv_chunk_operands(0, rf_ref[0], knf_ref[0], lwf_ref[0, 0], kdf_ref[0, 0], bbf_ref[0, 0], tm_ref[0]),
           _rwkv_chunk_operands(1, rb_ref[0], knb_ref[0], lwb_ref[0, 0], kdb_ref[0, 0], bbb_ref[0, 0], tm_ref[1])]
    vals = [vf_ref[0], vb_ref[0]]
    outs = [yf_ref, yb_ref]
    units = [(d, p) for d in range(2) for p in range(RWKV_PAIRS)]

    def part(d, name, p):
        return ops[d][name][:, p * W:(p + 1) * W]

    def rows(d, p, first, second):
        return jnp.concatenate([part(d, first, p), part(d, second, p)], axis=0).astype(BF16)

    def keys(d, p):
        return jnp.concatenate([_pair_diag(part(d, "b_til", p).astype(BF16), low),
                                _pair_diag(part(d, "k_til", p).astype(BF16), low)], axis=0)

    ar = [rows(d, p, "a_hat", "r_til") for d, p in units]
    pm = [_dot_nt(ar[i], keys(d, p)) for i, (d, p) in enumerate(units)]
    state = [s_ref[d, p] for d, p in units]
    from_state = [_dot_nt(ar[i], state[i].astype(BF16)) for i in range(len(units))]
    n_ab = [jnp.where(strict[d], pm[i][0:C, 0:W], 0.0) for i, (d, p) in enumerate(units)]
    a_ak = [jnp.where(strict[d], pm[i][0:C, W:2 * W], 0.0) for i, (d, p) in enumerate(units)]
    a_r = [jnp.where(incl2[d], pm[i][C:2 * C, :], 0.0) for i, (d, p) in enumerate(units)]
    v16 = [vals[d][:, p * W:(p + 1) * W].astype(BF16) for d, p in units]
    v_diag = [_pair_diag(t, low) for t in v16]
    rhs = [from_state[i][0:C] + _dot(a_ak[i].astype(BF16), v_diag[i]) for i in range(len(units))]
    u16 = [t.astype(BF16) for t in _unit_triangular_solve(n_ab, rhs, eye, blocks, low)]
    y = [from_state[i][C:2 * C]
         + _dot(a_r[i].astype(BF16), jnp.concatenate([_pair_diag(u16[i], low), v_diag[i]], axis=0))
         for i in range(len(units))]
    for i, (d, p) in enumerate(units):
        outs[d][0, :, p * W:(p + 1) * W] = y[i]
    upd = [_dot_tn(jnp.concatenate([u16[i], v16[i]], axis=0), rows(d, p, "b_out", "k_out"))
           for i, (d, p) in enumerate(units)]
    for i, (d, p) in enumerate(units):
        s_ref[d, p] = state[i] * part(d, "w_tot", p) + jnp.where(same_head, upd[i], 0.0)


def _rwkv_scan(r, v, kn, lw, kd, bb, tmats):
    b, l, d = r.shape
    C = RWKV_CHUNK
    nc = l // C

    fwd = pl.BlockSpec((1, C, d), lambda i, c: (i, c, 0))
    bwd = pl.BlockSpec((1, C, d), lambda i, c: (i, nc - 1 - c, 0))
    fwd_dir = pl.BlockSpec((1, 1, C, d), lambda i, c: (0, i, c, 0))
    bwd_dir = pl.BlockSpec((1, 1, C, d), lambda i, c: (1, i, nc - 1 - c, 0))
    out = jax.ShapeDtypeStruct((b, l, d), F32)
    return pl.pallas_call(
        _rwkv_scan_body,
        out_shape=(out, out),
        grid=(b, nc),
        in_specs=[fwd, fwd, fwd, bwd, bwd, bwd, fwd_dir, fwd_dir, fwd_dir, bwd_dir, bwd_dir, bwd_dir,
                  _const_spec((2, C, C))],
        out_specs=(fwd, bwd),
        scratch_shapes=[pltpu.VMEM((2, RWKV_PAIRS, RWKV_PAIR, RWKV_PAIR), F32)],
        compiler_params=_params("parallel", "arbitrary"),
        name="rwkv_scan",
    )(r, v, kn, r, v, kn, lw, kd, bb, lw, kd, bb, tmats)


def _rwkv_out_body(yf_ref, yb_ref, bonus_ref, gate_ref, x_ref, lg_ref, lb_ref, ones_ref, w_ref, g_ref, o_ref):
    ones_bd = ones_ref[...]
    y = yf_ref[0] + yb_ref[0]
    inv_n = 1.0 / RWKV_HEAD_DIM
    yc = y - _segsum64(y, ones_bd) * inv_n
    yn = yc * lax.rsqrt(_segsum64(yc * yc, ones_bd) * inv_n + RWKV_GN_EPS)
    yn = yn * lg_ref[...] + lb_ref[...] + bonus_ref[0]
    out = _dot((yn * gate_ref[0]).astype(BF16), w_ref[...])
    o_ref[0] = x_ref[0] + _rms(out, g_ref[...], NORM_EPS)


def _rwkv_out(yf, yb, bonus, gate, x, p, g):
    b, l, d = x.shape
    tm = _tile(l, 256)
    tok = pl.BlockSpec((1, tm, d), lambda i, j: (i, j, 0))
    return pl.pallas_call(
        _rwkv_out_body,
        out_shape=jax.ShapeDtypeStruct((b, l, d), F32),
        grid=(b, l // tm),
        in_specs=[tok, tok, tok, tok, tok,
                  _const_spec((1, d)), _const_spec((1, d)), _const_spec(p["ones_bd"].shape),
                  _resident_spec(p["w_out"].shape), _const_spec((1, d))],
        out_specs=tok,
        compiler_params=_params("parallel", "parallel"),
        name="rwkv_out",
    )(yf, yb, bonus, gate, x, p["ln_g"], p["ln_b"], p["ones_bd"], p["w_out"], g)


def _rwkv_layer(x, ng, p):
    r, v, kn, gate, bonus, lw, kd, bb = _rwkv_in(x, ng[2:3], p)
    yf, yb = _rwkv_scan(r, v, kn, lw, kd, bb, p["tmats"])
    return _rwkv_out(yf, yb, bonus, gate, x, p, ng[3:4])


def _rwkv_prepare(j, mix, w_rkv, w0, w1, w2, a0, a1, a2, g1, g2, k_k, k_a, r_k, ln_g, ln_b, w_out):
    def cat_lora_in(w):
        return jnp.concatenate([w[0], w[1]], axis=1).astype(BF16)

    def pad_lora_out(w):
        zero = jnp.zeros_like(w[0])
        return jnp.stack([jnp.concatenate([w[0], zero], axis=0),
                          jnp.concatenate([zero, w[1]], axis=0)]).astype(BF16)

    seg = np.arange(256) // RWKV_HEAD_DIM
    ones_bd = jnp.asarray(seg[:, None] == seg[None, :], dtype=BF16)
    tril = jnp.tril(jnp.ones((RWKV_CHUNK, RWKV_CHUNK), F32))
    return dict(mix=mix[j], w_rkv=w_rkv[j].astype(BF16), w0=w0[j], w1=cat_lora_in(w1[j]), w2=pad_lora_out(w2[j]),
                a0=a0[j], a1=cat_lora_in(a1[j]), a2=pad_lora_out(a2[j]),
                g1=g1[j].astype(BF16), g2=g2[j].astype(BF16),
                k_k=k_k[j][None, :], k_a=k_a[j][None, :], r_k=r_k[j].reshape(1, D_MODEL),
                ln_g=ln_g[j][None, :], ln_b=ln_b[j][None, :], w_out=w_out[j].astype(BF16),
                ones_bd=ones_bd, tmats=jnp.stack([tril, tril.T]).astype(BF16))


def _trunk(x, mem, layers):
    for lay in layers:
        ng = lay["ng"]
        x = _ffn(x, ng[0:2], lay["ffn_in0"], lay["ffn_out0"])
        x = lay["mixer"](x, ng, lay["mixer_params"])
        kv = _memkv(mem, lay["mem_g"], lay["wkv"])
        x = _xattn(x, ng[4:6], kv, lay["wq"], lay["wo"])
        x = _ffn(x, ng[6:8], lay["ffn_in1"], lay["ffn_out1"])
    return x


def kernel(x_prompt, x_sample, mem_prompt, mem_sample, norm_g, mem_norm_g, ffn_w_in, ffn_w_out, xa_wq, xa_wkv, xa_wo, ssd_w_in, ssd_conv_w, ssd_conv_b, ssd_dt_bias, ssd_a_log, ssd_d, ssd_norm_g, ssd_w_out, rwkv_mix, rwkv_w_rkv, rwkv_w0, rwkv_w1, rwkv_w2, rwkv_a0, rwkv_a1, rwkv_a2, rwkv_g1, rwkv_g2, rwkv_k_k, rwkv_k_a, rwkv_r_k, rwkv_ln_g, rwkv_ln_b, rwkv_w_out, ret_w_in, ret_gn_g, ret_gn_b, ret_w_out):
    assert x_prompt.shape[1] == x_sample.shape[1]
    seq = x_prompt.shape[1]
    layers = []
    for i in range(DEPTH):
        kind, j = i % 3, i // 3
        if kind == 0:
            mixer = _ssd_layer
            mp = _ssd_prepare(j, ssd_w_in, ssd_conv_w, ssd_conv_b, ssd_dt_bias, ssd_a_log, ssd_d,
                              ssd_norm_g, ssd_w_out)
        elif kind == 1:
            mixer = _rwkv_layer
            mp = _rwkv_prepare(j, rwkv_mix, rwkv_w_rkv, rwkv_w0, rwkv_w1, rwkv_w2, rwkv_a0, rwkv_a1, rwkv_a2,
                               rwkv_g1, rwkv_g2, rwkv_k_k, rwkv_k_a, rwkv_r_k, rwkv_ln_g, rwkv_ln_b, rwkv_w_out)
        else:
            mixer = _ret_layer
            mp = _ret_prepare(j, seq, ret_w_in, ret_gn_g, ret_gn_b, ret_w_out)
        layers.append(dict(
            ng=norm_g[i], mem_g=mem_norm_g[i][None, :], mixer=mixer, mixer_params=mp,
            ffn_in0=ffn_w_in[i, 0].astype(BF16), ffn_out0=ffn_w_out[i, 0].astype(BF16),
            ffn_in1=ffn_w_in[i, 1].astype(BF16), ffn_out1=ffn_w_out[i, 1].astype(BF16),
            wq=xa_wq[i].astype(BF16), wkv=xa_wkv[i].astype(BF16), wo=xa_wo[i].astype(BF16)))
    return (_trunk(x_prompt, mem_prompt, layers), _trunk(x_sample, mem_sample, layers))
```
